```python
import jax, jax.numpy as jnp
from jax import lax
import numpy as np

D_MODEL = 1024
BATCH = 4
SEQ = 4096
DEPTH = 4

EXPAND = 2
D_MIX = EXPAND * D_MODEL
NORM_EPS = 1e-5

FOX_WIDTH = D_MIX // 2
FOX_HEAD_DIM = 128
FOX_HEADS = FOX_WIDTH // FOX_HEAD_DIM
FOX_BLOCK = 128

SSD_WIDTH = D_MIX - FOX_WIDTH
SSD_HEAD_DIM = 64
SSD_HEADS = SSD_WIDTH // SSD_HEAD_DIM
SSD_GROUPS = 2
SSD_HPG = SSD_HEADS // SSD_GROUPS
SSD_STATE = 128
SSD_CONV = 4
SSD_CHUNK = 128
SSD_CONV_DIM = SSD_WIDTH + 2 * SSD_GROUPS * SSD_STATE

HGRN_WIDTH = D_MIX
HGRN_KEY_DIM = 128
HGRN_HEADS = HGRN_WIDTH // HGRN_KEY_DIM
HGRN_VAL_DIM = HGRN_WIDTH // HGRN_HEADS
HGRN_CHUNK = 64

N_EVEN = (DEPTH + 1) // 2
N_ODD = DEPTH // 2

EVEN_SIZES = [FOX_WIDTH, FOX_WIDTH, FOX_WIDTH, FOX_WIDTH, FOX_HEADS,
              SSD_WIDTH, SSD_CONV_DIM, SSD_HEADS]
EVEN_IN = sum(EVEN_SIZES)
ODD_SIZES = [HGRN_WIDTH, HGRN_WIDTH, HGRN_WIDTH, HGRN_WIDTH]
ODD_IN = sum(ODD_SIZES)

kernel_name = "fox_ssd_hgrn2_hybrid_trunk"


def _split(y, sizes):
    offs = [int(o) for o in np.cumsum(sizes)[:-1]]
    return jnp.split(y, offs, axis=-1)


def rmsnorm(x, w):
    xf = x.astype(jnp.float32)
    y = xf * lax.rsqrt(jnp.mean(xf * xf, axis=-1, keepdims=True) + NORM_EPS)
    return (y * w.astype(jnp.float32)).astype(x.dtype)


def fox_attention(q, k, v, log_f):
    T, Dh = q.shape[1], q.shape[3]
    scale = Dh ** -0.5
    c = jnp.transpose(jnp.cumsum(log_f, axis=1), (0, 2, 1))
    outs = []
    for blk in range(T // FOX_BLOCK):
        s0 = blk * FOX_BLOCK
        s1 = s0 + FOX_BLOCK
        logits = jnp.einsum('bqhd,bkhd->bhqk', q[:, s0:s1], k[:, :s1]).astype(jnp.float32) * scale
        logits = logits + c[:, :, s0:s1, None] - c[:, :, None, :s1]
        causal = jnp.arange(s1)[None, :] <= jnp.arange(s0, s1)[:, None]
        logits = jnp.where(causal, logits, -jnp.inf)
        p = jax.nn.softmax(logits, axis=-1).astype(v.dtype)
        outs.append(jnp.einsum('bhqk,bkhd->bqhd', p, v[:, :s1]))
    return jnp.concatenate(outs, axis=1)


def causal_depthwise_conv(x, w, b):
    C = x.shape[-1]
    y = lax.conv_general_dilated(x, w.astype(x.dtype)[:, None, :], window_strides=(1,),
                                 padding=[(SSD_CONV - 1, 0)],
                                 dimension_numbers=('NWC', 'WIO', 'NWC'),
                                 feature_group_count=C)
    return y + b.astype(x.dtype)


def segsum(a):
    L = a.shape[-1]
    cs = jnp.cumsum(a, axis=-1)
    diff = cs[..., :, None] - cs[..., None, :]
    return jnp.where(jnp.tril(jnp.ones((L, L), dtype=bool)), diff, -jnp.inf)


def ssd_scan(x, dt, A, Bm, Cm):
    f32 = jnp.float32
    Bsz, T, G, J, P = x.shape
    N = Bm.shape[-1]
    Lc = SSD_CHUNK
    nc = T // Lc
    xd = (x.astype(f32) * dt[..., None]).reshape(Bsz, nc, Lc, G, J, P)
    a = (dt * A).reshape(Bsz, nc, Lc, G, J).transpose(0, 3, 4, 1, 2)
    Bc = Bm.astype(f32).reshape(Bsz, nc, Lc, G, N)
    Cc = Cm.astype(f32).reshape(Bsz, nc, Lc, G, N)
    a_cum = jnp.cumsum(a, axis=-1)
    Ldec = jnp.exp(segsum(a))
    cb = jnp.einsum('bclgn,bcsgn->bgcls', Cc, Bc)
    y_diag = jnp.einsum('bgjcls,bcsgjp->bclgjp', cb[:, :, None] * Ldec, xd)
    decay_states = jnp.exp(a_cum[..., -1:] - a_cum).transpose(0, 3, 4, 1, 2)
    states = jnp.einsum('bclgn,bclgjp->bcgjpn', Bc, xd * decay_states[..., None])
    chunk_a = jnp.pad(a_cum[..., -1], [(0, 0), (0, 0), (0, 0), (1, 0)])
    dec_chunk = jnp.exp(segsum(chunk_a))
    states = jnp.pad(states, [(0, 0), (1, 0), (0, 0), (0, 0), (0, 0), (0, 0)])
    new_states = jnp.einsum('bgjzc,bcgjpn->bzgjpn', dec_chunk, states)
    prev_states = new_states[:, :-1]
    out_decay = jnp.exp(a_cum).transpose(0, 3, 4, 1, 2)
    y_off = jnp.einsum('bclgn,bcgjpn->bclgjp', Cc, prev_states) * out_decay[..., None]
    return (y_diag + y_off).reshape(Bsz, T, G, J, P)


def hgrn2_chunk_scan(q, k, v, g):
    Bsz, T, H, K = q.shape
    V = v.shape[-1]
    Lc = HGRN_CHUNK
    nc = T // Lc

    def to_chunks(a):
        return a.reshape(Bsz, nc, Lc, H, a.shape[-1]).transpose(1, 0, 3, 2, 4)

    causal = jnp.tril(jnp.ones((Lc, Lc), dtype=bool))[:, :, None]

    def step(S, inp):
        qi, ki, vi, gi = inp
        bcum = jnp.cumsum(gi, axis=2)
        diff = bcum[:, :, :, None, :] - bcum[:, :, None, :, :]
        dec = jnp.exp(jnp.where(causal, diff, -jnp.inf))
        attn = jnp.einsum('bhtsk,bhsk->bhts', qi[:, :, :, None, :] * dec, ki)
        o = (jnp.einsum('bhts,bhsv->bhtv', attn, vi)
             + jnp.einsum('bhtk,bhkv->bhtv', qi * jnp.exp(bcum), S))
        b_last = bcum[:, :, -1:, :]
        S = (S * jnp.exp(b_last[:, :, 0, :])[..., None]
             + jnp.einsum('bhsk,bhsv->bhkv', ki * jnp.exp(b_last - bcum), vi))
        return S, o

    S0 = jnp.zeros((Bsz, H, K, V), jnp.float32)
    _, o = lax.scan(step, S0, (to_chunks(q), to_chunks(k), to_chunks(v), to_chunks(g)))
    return o.transpose(1, 0, 3, 2, 4).reshape(Bsz, T, H, V)


def hgrn_lower_bounds(lb_logits):
    p = jax.nn.softmax(lb_logits.astype(jnp.float32), axis=0)
    cs = jnp.cumsum(p, axis=0)
    return cs - cs[0:1]


def even_layer(u, w_in, w_out, f_bias, conv_w, conv_b, dt_bias, A_log, D_skip, ssd_norm_w):
    f32 = jnp.float32
    Bsz, T, _ = u.shape
    q, k, v, gate_a, f_logit, z, xbc, dt_raw = _split(u @ w_in.astype(u.dtype), EVEN_SIZES)
    hs = (Bsz, T, FOX_HEADS, FOX_HEAD_DIM)
    log_f = jax.nn.log_sigmoid(f_logit.astype(f32) + f_bias.astype(f32))
    o_a = fox_attention(q.reshape(hs), k.reshape(hs), v.reshape(hs), log_f)
    o_a = o_a.reshape(Bsz, T, FOX_WIDTH).astype(u.dtype) * jax.nn.silu(gate_a)
    xbc = jax.nn.silu(causal_depthwise_conv(xbc, conv_w, conv_b))
    xs, Bm, Cm = _split(xbc, [SSD_WIDTH, SSD_GROUPS * SSD_STATE, SSD_GROUPS * SSD_STATE])
    xs = xs.reshape(Bsz, T, SSD_GROUPS, SSD_HPG, SSD_HEAD_DIM)
    Bm = Bm.reshape(Bsz, T, SSD_GROUPS, SSD_STATE)
    Cm = Cm.reshape(Bsz, T, SSD_GROUPS, SSD_STATE)
    dt = jax.nn.softplus(dt_raw.astype(f32) + dt_bias.astype(f32)).reshape(Bsz, T, SSD_GROUPS, SSD_HPG)
    A = -jnp.exp(A_log.astype(f32)).reshape(SSD_GROUPS, SSD_HPG)
    y = ssd_scan(xs, dt, A, Bm, Cm) + D_skip.astype(f32).reshape(SSD_GROUPS, SSD_HPG)[:, :, None] * xs.astype(f32)
    y = y.reshape(Bsz, T, SSD_WIDTH).astype(u.dtype) * jax.nn.silu(z)
    y = rmsnorm(y.reshape(Bsz, T, SSD_GROUPS, SSD_WIDTH // SSD_GROUPS),
                ssd_norm_w.reshape(SSD_GROUPS, SSD_WIDTH // SSD_GROUPS)).reshape(Bsz, T, SSD_WIDTH)
    return jnp.concatenate([o_a, y], axis=-1) @ w_out.astype(u.dtype)


def odd_layer(u, w_in, w_out, lb, norm_w):
    f32 = jnp.float32
    Bsz, T, _ = u.shape
    q, f, i, gate = _split(u @ w_in.astype(u.dtype), ODD_SIZES)
    ks = (Bsz, T, HGRN_HEADS, HGRN_KEY_DIM)
    ff = f.astype(f32)
    log_f = jnp.logaddexp(jnp.log(lb), jnp.log1p(-lb) + jax.nn.log_sigmoid(ff))
    k_in = (1.0 - lb) * jax.nn.sigmoid(-ff)
    qf = jax.nn.silu(q.astype(f32))
    o = hgrn2_chunk_scan(qf.reshape(ks), k_in.reshape(ks), i.astype(f32).reshape(Bsz, T, HGRN_HEADS, HGRN_VAL_DIM),
                         log_f.reshape(ks))
    o = rmsnorm(o, norm_w).reshape(Bsz, T, HGRN_WIDTH).astype(u.dtype) * jax.nn.silu(gate)
    return o @ w_out.astype(u.dtype)


def setup_inputs(seed: int = 0) -> dict:
    key = jax.random.key(seed)
    ks = jax.random.split(key, 20)
    f32 = jnp.float32
    nrm = lambda k, s: jax.random.normal(k, s, f32)
    x = nrm(ks[0], (BATCH, SEQ, D_MODEL))
    norm_w = 1.0 + 0.02 * nrm(ks[1], (DEPTH, D_MODEL))
    final_norm_w = 1.0 + 0.02 * nrm(ks[2], (D_MODEL,))
    even_w_in = nrm(ks[3], (N_EVEN, D_MODEL, EVEN_IN)) * D_MODEL ** -0.5
    even_w_out = nrm(ks[4], (N_EVEN, D_MIX, D_MODEL)) * D_MIX ** -0.5
    fox_f_bias = 2.0 + 0.5 * nrm(ks[5], (N_EVEN, FOX_HEADS))
    ssd_conv_w = nrm(ks[6], (N_EVEN, SSD_CONV, SSD_CONV_DIM)) * SSD_CONV ** -0.5
    ssd_conv_b = 0.02 * nrm(ks[7], (N_EVEN, SSD_CONV_DIM))
    log_dt = jax.random.uniform(ks[8], (N_EVEN, SSD_HEADS), f32, np.log(1e-3), np.log(1e-1))
    dt0 = jnp.exp(log_dt)
    ssd_dt_bias = dt0 + jnp.log(-jnp.expm1(-dt0))
    ssd_A_log = jnp.log(jax.random.uniform(ks[9], (N_EVEN, SSD_HEADS), f32, 1.0, 16.0))
    ssd_D = 1.0 + 0.1 * nrm(ks[10], (N_EVEN, SSD_HEADS))
    ssd_norm_w = 1.0 + 0.02 * nrm(ks[11], (N_EVEN, SSD_WIDTH))
    odd_w_in = nrm(ks[12], (N_ODD, D_MODEL, ODD_IN)) * D_MODEL ** -0.5
    odd_w_out = nrm(ks[13], (N_ODD, HGRN_WIDTH, D_MODEL)) * HGRN_WIDTH ** -0.5
    hgrn_lb_logits = 0.5 * nrm(ks[14], (N_ODD, HGRN_WIDTH))
    hgrn_norm_w = 1.0 + 0.02 * nrm(ks[15], (N_ODD, HGRN_VAL_DIM))
    return {"x": x, "norm_w": norm_w, "final_norm_w": final_norm_w,
            "even_w_in": even_w_in, "even_w_out": even_w_out, "fox_f_bias": fox_f_bias,
            "ssd_conv_w": ssd_conv_w, "ssd_conv_b": ssd_conv_b, "ssd_dt_bias": ssd_dt_bias,
            "ssd_A_log": ssd_A_log, "ssd_D": ssd_D, "ssd_norm_w": ssd_norm_w,
            "odd_w_in": odd_w_in, "odd_w_out": odd_w_out,
            "hgrn_lb_logits": hgrn_lb_logits, "hgrn_norm_w": hgrn_norm_w}


def reference(x, norm_w, final_norm_w, even_w_in, even_w_out, fox_f_bias, ssd_conv_w, ssd_conv_b,
              ssd_dt_bias, ssd_A_log, ssd_D, ssd_norm_w, odd_w_in, odd_w_out, hgrn_lb_logits, hgrn_norm_w):
    lbs = hgrn_lower_bounds(hgrn_lb_logits)
    h = x
    for layer in range(DEPTH):
        u = rmsnorm(h, norm_w[layer])
        if layer % 2 == 0:
            e = layer // 2
            h = h + even_layer(u, even_w_in[e], even_w_out[e], fox_f_bias[e], ssd_conv_w[e], ssd_conv_b[e],
                               ssd_dt_bias[e], ssd_A_log[e], ssd_D[e], ssd_norm_w[e])
        else:
            o = layer // 2
            h = h + odd_layer(u, odd_w_in[o], odd_w_out[o], lbs[o].astype(jnp.float32), hgrn_norm_w[o])
    return rmsnorm(h, final_norm_w)
```

```python
import functools

import jax
import jax.numpy as jnp
from jax import lax
from jax.experimental import pallas as pl
from jax.experimental.pallas import tpu as pltpu

F32 = jnp.float32
BF16 = jnp.bfloat16

D_MODEL = 1024
D_MIX = 2048
NORM_EPS = 1e-5

FOX_WIDTH = 1024
FOX_HEAD_DIM = 128
FOX_HEADS = 8

SSD_WIDTH = 1024
SSD_HEAD_DIM = 64
SSD_HEADS = 16
SSD_GROUPS = 2
SSD_HPG = 8
SSD_STATE = 128
SSD_CONV = 4
SSD_CHUNK = 128
SSD_GW = SSD_WIDTH // SSD_GROUPS

HGRN_WIDTH = 2048
HGRN_DK = 128
HGRN_HEADS = 16
HGRN_CHUNK = 64
HGRN_SUB = 16

LANE = 128
VMEM_LIMIT = 48 * 1024 * 1024

EV_Q, EV_K, EV_V, EV_GA, EV_Z, EV_X = 0, 1024, 2048, 3072, 4096, 5120
EV_B = EV_X + SSD_WIDTH
EV_C = EV_B + SSD_GROUPS * SSD_STATE
EV_COLS = EV_C + SSD_GROUPS * SSD_STATE
OD_Q, OD_I, OD_G = 0, 2048, 4096
OD_COLS = 6144


def _cparams(*sem):
    return pltpu.CompilerParams(dimension_semantics=sem, vmem_limit_bytes=VMEM_LIMIT)


def _dot(a, b):
    return jnp.dot(a, b, preferred_element_type=F32)


def _dot_nt(a, b):
    return lax.dot_general(a, b, (((1,), (1,)), ((), ())), preferred_element_type=F32)


def _dot_tn(a, b):
    return lax.dot_general(a, b, (((0,), (0,)), ((), ())), preferred_element_type=F32)


def _split3(x):
    hi = x.astype(BF16)
    r1 = x - hi.astype(F32)
    mid = r1.astype(BF16)
    lo = (r1 - mid.astype(F32)).astype(BF16)
    return hi, mid, lo


def _dot01_l(m01, x):
    hi, mid, lo = _split3(x)
    return _dot(m01, hi) + _dot(m01, mid) + _dot(m01, lo)


def _dot01_r(x, m01):
    hi, mid, lo = _split3(x)
    return _dot(hi, m01) + _dot(mid, m01) + _dot(lo, m01)


def _log_sigmoid(x):
    return jnp.minimum(x, 0.0) - jnp.log(1.0 + jnp.exp(-jnp.abs(x)))


def _softplus(x):
    return jnp.maximum(x, 0.0) + jnp.log(1.0 + jnp.exp(-jnp.abs(x)))


def _sigmoid(x):
    return 1.0 / (1.0 + jnp.exp(-x))


def _silu(x):
    return x * _sigmoid(x)


def _iota2(shape, dim):
    return lax.broadcasted_iota(jnp.int32, shape, dim)


def _rms_kernel(x_ref, w_ref, o_ref):
    x = x_ref[...]
    ms = jnp.mean(x * x, axis=-1, keepdims=True)
    o_ref[...] = (x * lax.rsqrt(ms + NORM_EPS) * w_ref[...]).astype(o_ref.dtype)


def _rmsnorm(x, w, tm=512):
    n, d = x.shape
    return pl.pallas_call(
        _rms_kernel,
        out_shape=jax.ShapeDtypeStruct((n, d), BF16),
        grid=(n // tm,),
        in_specs=[pl.BlockSpec((tm, d), lambda i: (i, 0)),
                  pl.BlockSpec((1, d), lambda i: (0, 0))],
        out_specs=pl.BlockSpec((tm, d), lambda i: (i, 0)),
        compiler_params=_cparams("parallel"),
        name="rmsnorm",
    )(x, w.reshape(1, d))


def _mm_kernel(a_ref, w_ref, o_ref):
    o_ref[...] = _dot(a_ref[...], w_ref[...]).astype(o_ref.dtype)


def _matmul(a, w, out_dtype, tm, tn, name):
    n, k = a.shape
    c = w.shape[1]
    return pl.pallas_call(
        _mm_kernel,
        out_shape=jax.ShapeDtypeStruct((n, c), out_dtype),
        grid=(n // tm, c // tn),
        in_specs=[pl.BlockSpec((tm, k), lambda i, j: (i, 0)),
                  pl.BlockSpec((k, tn), lambda i, j: (0, j))],
        out_specs=pl.BlockSpec((tm, tn), lambda i, j: (i, j)),
        compiler_params=_cparams("parallel", "arbitrary"),
        name=name,
    )(a, w)


def _small_proj_kernel(a_ref, w_ref, wt_ref, o_ref, ot_ref):
    a = a_ref[...]
    o_ref[...] = _dot(a, w_ref[...])
    ot_ref[...] = _dot_nt(wt_ref[...], a)


def _small_proj(a, w, wt, tm=512):
    n, k = a.shape
    c = w.shape[1]
    return pl.pallas_call(
        _small_proj_kernel,
        out_shape=(jax.ShapeDtypeStruct((n, c), F32), jax.ShapeDtypeStruct((32, n), F32)),
        grid=(n // tm,),
        in_specs=[pl.BlockSpec((tm, k), lambda i: (i, 0)),
                  pl.BlockSpec((k, c), lambda i: (0, 0)),
                  pl.BlockSpec((32, k), lambda i: (0, 0))],
        out_specs=(pl.BlockSpec((tm, c), lambda i: (i, 0)),
                   pl.BlockSpec((32, tm), lambda i: (0, i))),
        compiler_params=_cparams("parallel"),
        name="gate_proj",
    )(a, w, wt)


def _out_kernel(*refs, n_in, final):
    a_refs = refs[:n_in]
    w_ref, h_ref, nw_ref = refs[n_in:n_in + 3]
    outs = refs[n_in + 3:]
    acc = h_ref[...]
    k0 = 0
    for a_ref in a_refs:
        kk = a_ref.shape[1]
        acc = acc + _dot(a_ref[...], w_ref[k0:k0 + kk, :])
        k0 += kk
    ms = jnp.mean(acc * acc, axis=-1, keepdims=True)
    normed = acc * lax.rsqrt(ms + NORM_EPS) * nw_ref[...]
    if final:
        outs[0][...] = normed
    else:
        outs[0][...] = acc
        outs[1][...] = normed.astype(BF16)


def _out_proj(acts, w, h, next_norm_w, final, tm=512):
    n, d = h.shape
    kt = w.shape[0]
    in_specs = [pl.BlockSpec((tm, a.shape[1]), lambda i: (i, 0)) for a in acts]
    in_specs += [pl.BlockSpec((kt, d), lambda i: (0, 0)),
                 pl.BlockSpec((tm, d), lambda i: (i, 0)),
                 pl.BlockSpec((1, d), lambda i: (0, 0))]
    row = pl.BlockSpec((tm, d), lambda i: (i, 0))
    if final:
        out_shape = jax.ShapeDtypeStruct((n, d), F32)
        out_specs = row
    else:
        out_shape = (jax.ShapeDtypeStruct((n, d), F32), jax.ShapeDtypeStruct((n, d), BF16))
        out_specs = (row, row)
    return pl.pallas_call(
        functools.partial(_out_kernel, n_in=len(acts), final=final),
        out_shape=out_shape,
        grid=(n // tm,),
        in_specs=in_specs,
        out_specs=out_specs,
        compiler_params=_cparams("parallel"),
        name="out_proj",
    )(*acts, w, h, next_norm_w.reshape(1, d))


def _foxgate_kernel(s_ref, st_ref, bc_ref, br_ref, cc_ref, cr_ref, carry_c, carry_r, *, tc):
    @pl.when(pl.program_id(1) == 0)
    def _():
        carry_c[...] = jnp.zeros_like(carry_c)
        carry_r[...] = jnp.zeros_like(carry_r)

    ri = _iota2((tc, tc), 0)
    ci = _iota2((tc, tc), 1)
    ltri = (ci <= ri).astype(BF16)
    utri = (ri <= ci).astype(BF16)

    lc = _log_sigmoid(s_ref[...] + bc_ref[...])
    cc = _dot01_l(ltri, lc) + carry_c[0:1, :]
    cc_ref[...] = cc
    carry_c[...] = jnp.broadcast_to(cc[tc - 1:tc, :], carry_c.shape)

    lr = _log_sigmoid(st_ref[0:8, :] + br_ref[...])
    cr = _dot01_r(lr, utri) + carry_r[:, 0:1]
    cr_ref[...] = cr
    carry_r[...] = jnp.broadcast_to(cr[:, tc - 1:tc], carry_r.shape)


def _fox_gate_cumsum(small, small_t, f_bias, bsz, seq, tc=512):
    n = small.shape[0]
    nt = seq // tc
    bias_c = jnp.zeros((1, LANE), F32).at[0, :FOX_HEADS].set(f_bias.astype(F32))
    bias_r = jnp.broadcast_to(f_bias.astype(F32)[:, None], (FOX_HEADS, tc))
    return pl.pallas_call(
        functools.partial(_foxgate_kernel, tc=tc),
        out_shape=(jax.ShapeDtypeStruct((n, LANE), F32), jax.ShapeDtypeStruct((FOX_HEADS, n), F32)),
        grid=(bsz, nt),
        in_specs=[pl.BlockSpec((tc, LANE), lambda b, j: (b * nt + j, 0)),
                  pl.BlockSpec((32, tc), lambda b, j: (0, b * nt + j)),
                  pl.BlockSpec((1, LANE), lambda b, j: (0, 0)),
                  pl.BlockSpec((FOX_HEADS, tc), lambda b, j: (0, 0))],
        out_specs=(pl.BlockSpec((tc, LANE), lambda b, j: (b * nt + j, 0)),
                   pl.BlockSpec((FOX_HEADS, tc), lambda b, j: (0, b * nt + j))),
        scratch_shapes=[pltpu.VMEM((8, LANE), F32), pltpu.VMEM((8, LANE), F32)],
        compiler_params=_cparams("parallel", "arbitrary"),
        name="fox_gate_cumsum",
    )(small, small_t, bias_c, bias_r)


def _fox_kernel(q_ref, k_ref, v_ref, g_ref, cc_ref, cr_ref, o_ref, *, tq, scale):
    h = pl.program_id(1)
    qi = pl.program_id(2)
    q = q_ref[...]
    lane = _iota2((tq, LANE), 1)
    cq = jnp.sum(jnp.where(lane == h, cc_ref[...], 0.0), axis=-1, keepdims=True)

    def block(kj, carry, diagonal):
        m, l, acc = carry
        start = pl.multiple_of(kj * tq, tq)
        k = k_ref[pl.ds(start, tq), :]
        v = v_ref[pl.ds(start, tq), :]
        ck = cr_ref[kj]
        s = _dot_nt(q, k) * scale + (cq - ck)
        if diagonal:
            s = jnp.where(_iota2((tq, tq), 0) >= _iota2((tq, tq), 1), s, -jnp.inf)
        m_new = jnp.maximum(m, jnp.max(s, axis=-1, keepdims=True))
        alpha = jnp.exp(m - m_new)
        p = jnp.exp(s - m_new)
        l = alpha * l + jnp.sum(p, axis=-1, keepdims=True)
        acc = alpha * acc + _dot(p.astype(BF16), v)
        return m_new, l, acc

    init = (jnp.full((tq, 1), -1e30, F32), jnp.zeros((tq, 1), F32), jnp.zeros((tq, FOX_HEAD_DIM), F32))
    carry = lax.fori_loop(0, qi, lambda kj, c: block(kj, c, False), init)
    _, l, acc = block(qi, carry, True)
    o_ref[...] = (acc / l * _silu(g_ref[...].astype(F32))).astype(o_ref.dtype)


def _fox_attention(big, c_col, c_row, bsz, seq, tq=512):
    n = big.shape[0]
    nq = seq // tq
    qb, kb, vb, gb = EV_Q // LANE, EV_K // LANE, EV_V // LANE, EV_GA // LANE
    return pl.pallas_call(
        functools.partial(_fox_kernel, tq=tq, scale=FOX_HEAD_DIM ** -0.5),
        out_shape=jax.ShapeDtypeStruct((n, FOX_WIDTH), BF16),
        grid=(bsz, FOX_HEADS, nq),
        in_specs=[pl.BlockSpec((tq, LANE), lambda b, h, i: (b * nq + i, qb + h)),
                  pl.BlockSpec((seq, LANE), lambda b, h, i: (b, kb + h)),
                  pl.BlockSpec((seq, LANE), lambda b, h, i: (b, vb + h)),
                  pl.BlockSpec((tq, LANE), lambda b, h, i: (b * nq + i, gb + h)),
                  pl.BlockSpec((tq, LANE), lambda b, h, i: (b * nq + i, 0)),
                  pl.BlockSpec((None, nq, 1, tq), lambda b, h, i: (h, b, 0, 0))],
        out_specs=pl.BlockSpec((tq, LANE), lambda b, h, i: (b * nq + i, h)),
        compiler_params=_cparams("parallel", "parallel", "arbitrary"),
        name="fox_attention",
    )(big, big, big, big, c_col, c_row.reshape(FOX_HEADS, n // tq, 1, tq))


def _ssd_kernel(x_ref, b_ref, c_ref, z_ref, dtc_ref, dtr_ref,
                wx_ref, wb_ref, wc_ref, bx_ref, bb_ref, bc_ref,
                dbc_ref, dbr_ref, alc_ref, alr_ref, e_ref, e128_ref, dsk_ref, nw_ref,
                o_ref, pad_ref, st_ref, y_ref):
    lc = SSD_CHUNK
    gw = SSD_GW
    cw = gw + 2 * SSD_STATE

    @pl.when(pl.program_id(2) == 0)
    def _():
        pad_ref[0:8, :] = jnp.zeros((8, cw), F32)
        st_ref[...] = jnp.zeros_like(st_ref)

    pad_ref[8:8 + lc, 0:gw] = x_ref[...].astype(F32)
    pad_ref[8:8 + lc, gw:gw + SSD_STATE] = b_ref[...].astype(F32)
    pad_ref[8:8 + lc, gw + SSD_STATE:cw] = c_ref[...].astype(F32)
    w_all = jnp.concatenate([wx_ref[...], wb_ref[...], wc_ref[...]], axis=1)
    b_all = jnp.concatenate([bx_ref[...], bb_ref[...], bc_ref[...]], axis=1)
    conv = b_all + w_all[0:1, :] * pad_ref[5:5 + lc, :]
    for kk in range(1, SSD_CONV):
        conv = conv + w_all[kk:kk + 1, :] * pad_ref[5 + kk:5 + kk + lc, :]
    pad_ref[0:8, :] = pad_ref[lc:lc + 8, :]
    conv = _silu(conv)
    xc = conv[:, 0:gw]
    bm = conv[:, gw:gw + SSD_STATE].astype(BF16)
    cm = conv[:, gw + SSD_STATE:cw].astype(BF16)

    ri = _iota2((lc, lc), 0)
    ci = _iota2((lc, lc), 1)
    causal = ri >= ci
    ltri = causal.astype(BF16)
    utri = (ri <= ci).astype(BF16)

    dt_c = _softplus(dtc_ref[...] + dbc_ref[...])
    a_c = dt_c * (-jnp.exp(alc_ref[...]))
    acum_c = _dot01_l(ltri, a_c)
    dt_r = _softplus(dtr_ref[...] + dbr_ref[...])
    a_r = dt_r * (-jnp.exp(alr_ref[...]))
    acum_r = _dot01_r(a_r, utri)

    e = e_ref[...]
    dt_e = _dot01_r(dt_c, e)
    acum_e = _dot01_r(acum_c, e)
    acol = _dot01_r(acum_c, e128_ref[...])
    alast_e = acum_e[lc - 1:lc, :]

    xd = xc * dt_e
    xd_b = xd.astype(BF16)
    cb = _dot_nt(cm, bm)
    for j in range(SSD_HPG):
        seg = acol[:, j * LANE:(j + 1) * LANE] - acum_r[j:j + 1, :]
        ldec = jnp.exp(jnp.where(causal, seg, -jnp.inf))
        mj = (cb * ldec).astype(BF16)
        y_ref[:, j * SSD_HEAD_DIM:(j + 1) * SSD_HEAD_DIM] = _dot(
            mj, xd_b[:, j * SSD_HEAD_DIM:(j + 1) * SSD_HEAD_DIM])

    st_prev = st_ref[...]
    y_off = _dot(cm, st_prev.astype(BF16)) * jnp.exp(acum_e)
    wgt = (xd * jnp.exp(alast_e - acum_e)).astype(BF16)
    st_ref[...] = st_prev * jnp.exp(alast_e) + _dot_tn(bm, wgt)

    y = y_ref[...] + y_off + dsk_ref[...] * xc
    y = y * _silu(z_ref[...].astype(F32))
    ms = jnp.mean(y * y, axis=-1, keepdims=True)
    o_ref[...] = (y * lax.rsqrt(ms + NORM_EPS) * nw_ref[...]).astype(o_ref.dtype)


def _ssd(big, small, small_t, conv_w, conv_b, dt_bias, a_log, d_skip, norm_w, bsz, seq):
    n = big.shape[0]
    lc = SSD_CHUNK
    nc = seq // lc
    gw = SSD_GW
    f32 = lambda a: a.astype(F32)
    def col_layout(vec):
        out = jnp.zeros((SSD_GROUPS, 1, LANE), F32)
        return out.at[:, 0, :SSD_HPG].set(f32(vec).reshape(SSD_GROUPS, SSD_HPG))
    def row_layout(vec):
        return jnp.broadcast_to(f32(vec).reshape(SSD_GROUPS, SSD_HPG, 1), (SSD_GROUPS, SSD_HPG, lc))
    expand = (jnp.arange(LANE)[:, None] == (jnp.arange(gw) // SSD_HEAD_DIM)[None, :]).astype(BF16)
    expand128 = (jnp.arange(LANE)[:, None] == (jnp.arange(SSD_HPG * LANE) // LANE)[None, :]).astype(BF16)
    xb = EV_X // gw
    bb = EV_B // SSD_STATE
    cb = EV_C // SSD_STATE
    zb = EV_Z // gw
    row = lambda b, g, c: b * nc + c
    cw_x = lambda b, g, c: (0, g)
    cw_b = lambda b, g, c: (0, SSD_WIDTH // SSD_STATE + g)
    cw_c = lambda b, g, c: (0, SSD_WIDTH // SSD_STATE + SSD_GROUPS + g)
    gsel3 = lambda b, g, c: (g, 0, 0)
    return pl.pallas_call(
        _ssd_kernel,
        out_shape=jax.ShapeDtypeStruct((n, SSD_WIDTH), BF16),
        grid=(bsz, SSD_GROUPS, nc),
        in_specs=[
            pl.BlockSpec((lc, gw), lambda b, g, c: (row(b, g, c), xb + g)),
            pl.BlockSpec((lc, SSD_STATE), lambda b, g, c: (row(b, g, c), bb + g)),
            pl.BlockSpec((lc, SSD_STATE), lambda b, g, c: (row(b, g, c), cb + g)),
            pl.BlockSpec((lc, gw), lambda b, g, c: (row(b, g, c), zb + g)),
            pl.BlockSpec((lc, LANE), lambda b, g, c: (row(b, g, c), 1 + g)),
            pl.BlockSpec((8, lc), lambda b, g, c: (1 + g, row(b, g, c))),
            pl.BlockSpec((SSD_CONV, gw), cw_x),
            pl.BlockSpec((SSD_CONV, SSD_STATE), cw_b),
            pl.BlockSpec((SSD_CONV, SSD_STATE), cw_c),
            pl.BlockSpec((1, gw), cw_x),
            pl.BlockSpec((1, SSD_STATE), cw_b),
            pl.BlockSpec((1, SSD_STATE), cw_c),
            pl.BlockSpec((None, 1, LANE), gsel3),
            pl.BlockSpec((None, SSD_HPG, lc), gsel3),
            pl.BlockSpec((None, 1, LANE), gsel3),
            pl.BlockSpec((None, SSD_HPG, lc), gsel3),
            pl.BlockSpec((LANE, gw), lambda b, g, c: (0, 0)),
            pl.BlockSpec((LANE, SSD_HPG * LANE), lambda b, g, c: (0, 0)),
            pl.BlockSpec((1, gw), lambda b, g, c: (0, g)),
            pl.BlockSpec((1, gw), lambda b, g, c: (0, g)),
        ],
        out_specs=pl.BlockSpec((lc, gw), lambda b, g, c: (row(b, g, c), g)),
        scratch_shapes=[pltpu.VMEM((lc + 8, gw + 2 * SSD_STATE), F32),
                        pltpu.VMEM((SSD_STATE, gw), F32),
                        pltpu.VMEM((lc, gw), F32)],
        compiler_params=_cparams("parallel", "parallel", "arbitrary"),
        name="ssd_scan",
    )(big, big, big, big, small, small_t,
      f32(conv_w), f32(conv_w), f32(conv_w),
      f32(conv_b).reshape(1, -1), f32(conv_b).reshape(1, -1), f32(conv_b).reshape(1, -1),
      col_layout(dt_bias), row_layout(dt_bias), col_layout(a_log), row_layout(a_log),
      expand, expand128,
      jnp.repeat(f32(d_skip), SSD_HEAD_DIM).reshape(1, SSD_WIDTH),
      f32(norm_w).reshape(1, SSD_WIDTH))


def _hgrn_kernel(q_ref, i_ref, gt_ref, f_ref, lbl_ref, nw_ref, o_ref, st_ref, *, tb, layer):
    lc = HGRN_CHUNK
    sub = HGRN_SUB
    nsub = lc // sub
    dk = HGRN_DK

    @pl.when(pl.program_id(2) == 0)
    def _():
        st_ref[...] = jnp.zeros_like(st_ref)

    lbl = lbl_ref[...].astype(F32)
    ex = jnp.exp(lbl - jnp.max(lbl, axis=0, keepdims=True))
    pr = ex / jnp.sum(ex, axis=0, keepdims=True)
    lb = jnp.zeros((1, dk), F32)
    for r in range(1, layer + 1):
        lb = lb + pr[r:r + 1, :]
    log_lb = jnp.log(lb)
    log_1mlb = jnp.log(1.0 - lb)

    ri = _iota2((lc, lc), 0)
    ci = _iota2((lc, lc), 1)
    sub_shift = sub.bit_length() - 1
    sub_start = jnp.left_shift(jnp.right_shift(ri, sub_shift), sub_shift)
    l_within = ((ci >= sub_start) & (ci <= ri)).astype(BF16)
    l_before = (ci < sub_start).astype(BF16)
    ncat = sub * (nsub * (nsub - 1) // 2)
    seg_of_col = jnp.zeros((lc, ncat), jnp.int32)
    off = 0
    for i in range(1, nsub):
        cols = _iota2((lc, ncat), 1)
        seg_of_col = jnp.where((cols >= off) & (cols < off + sub * i), i, seg_of_col)
        off += sub * i
    cat_mask = jnp.right_shift(_iota2((lc, ncat), 0), sub_shift) == seg_of_col
    ones_b = jnp.ones((dk, dk), BF16)
    tsub = _iota2((sub, dk), 0)

    def chunk(c, _):
        r0 = pl.multiple_of(c * lc, lc)
        qf = _silu(q_ref[pl.ds(r0, lc), :].astype(F32))
        ff = f_ref[pl.ds(r0, lc), :]
        v = i_ref[pl.ds(r0, lc), :].astype(F32)
        ea = jnp.exp(-jnp.abs(ff))
        log_sig = jnp.minimum(ff, 0.0) - jnp.log(1.0 + ea)
        sig_neg = jnp.where(ff >= 0.0, ea, 1.0) / (1.0 + ea)
        t2 = log_1mlb + log_sig
        mx = jnp.maximum(log_lb, t2)
        g = mx + jnp.log(1.0 + jnp.exp(jnp.minimum(log_lb, t2) - mx))
        kin = (1.0 - lb) * sig_neg

        bw = _dot01_l(l_within, g)
        beta = _dot01_l(l_before, g)
        bfull = bw + beta
        qd = qf * jnp.exp(bw)
        v_b = v.astype(BF16)

        st = st_ref[...]
        o = _dot_nt((qd * jnp.exp(beta)).astype(BF16), st.astype(BF16))
        blast = bfull[lc - 1:lc, :]
        kd = (kin * jnp.exp(blast - bfull)).astype(BF16)
        st_ref[...] = st * jnp.exp(blast) + _dot_tn(v_b, kd)

        kcat = []
        vcat = []
        for i in range(1, nsub):
            beta_i = beta[i * sub:i * sub + 1, :]
            kcat.append(kin[0:i * sub, :] * jnp.exp(beta_i - bfull[0:i * sub, :]))
            vcat.append(v_b[0:i * sub, :])
        kcat = jnp.concatenate(kcat, axis=0).astype(BF16)
        vcat = jnp.concatenate(vcat, axis=0)
        a_cat = jnp.where(cat_mask, _dot_nt(qd.astype(BF16), kcat), 0.0)
        o = o + _dot(a_cat.astype(BF16), vcat)

        o_diag = []
        for i in range(nsub):
            sl = slice(i * sub, (i + 1) * sub)
            q_s, k_s, b_s, v_s = qf[sl], kin[sl], bw[sl], v[sl]
            prods = []
            for s in range(sub):
                dec = jnp.exp(jnp.where(tsub >= s, b_s - b_s[s:s + 1, :], -jnp.inf))
                prods.append(q_s * k_s[s:s + 1, :] * dec)
            rs = _dot(jnp.concatenate(prods, axis=0).astype(BF16), ones_b)
            od = rs[0:sub, :] * v_s[0:1, :]
            for s in range(1, sub):
                od = od + rs[s * sub:(s + 1) * sub, :] * v_s[s:s + 1, :]
            o_diag.append(od)
        o = o + jnp.concatenate(o_diag, axis=0)

        ms = jnp.mean(o * o, axis=-1, keepdims=True)
        o = o * lax.rsqrt(ms + NORM_EPS) * nw_ref[...]
        o = o * _silu(gt_ref[pl.ds(r0, lc), :].astype(F32))
        o_ref[pl.ds(r0, lc), :] = o.astype(o_ref.dtype)
        return 0

    lax.fori_loop(0, tb // lc, chunk, 0)


def _hgrn(big, fgate, lb_logits, norm_w, layer, bsz, seq, tb=512):
    n = big.shape[0]
    nt = seq // tb
    nl = lb_logits.shape[0]
    qb, ib, gb = OD_Q // LANE, OD_I // LANE, OD_G // LANE
    row = lambda b, h, t: b * nt + t
    return pl.pallas_call(
        functools.partial(_hgrn_kernel, tb=tb, layer=layer),
        out_shape=jax.ShapeDtypeStruct((n, HGRN_WIDTH), BF16),
        grid=(bsz, HGRN_HEADS, nt),
        in_specs=[pl.BlockSpec((tb, LANE), lambda b, h, t: (row(b, h, t), qb + h)),
                  pl.BlockSpec((tb, LANE), lambda b, h, t: (row(b, h, t), ib + h)),
                  pl.BlockSpec((tb, LANE), lambda b, h, t: (row(b, h, t), gb + h)),
                  pl.BlockSpec((tb, LANE), lambda b, h, t: (row(b, h, t), h)),
                  pl.BlockSpec((nl, LANE), lambda b, h, t: (0, h)),
                  pl.BlockSpec((1, LANE), lambda b, h, t: (0, 0))],
        out_specs=pl.BlockSpec((tb, LANE), lambda b, h, t: (row(b, h, t), h)),
        scratch_shapes=[pltpu.VMEM((HGRN_DK, HGRN_DK), F32)],
        compiler_params=_cparams("parallel", "parallel", "arbitrary"),
        name="hgrn2_scan",
    )(big, big, big, fgate, lb_logits.astype(F32), norm_w.astype(F32).reshape(1, LANE))


def _even_weights(w_in):
    w = w_in.astype(F32)
    q, k, v, ga = (w[:, i * 1024:(i + 1) * 1024] for i in range(4))
    fl = w[:, 4096:4104]
    z = w[:, 4104:5128]
    xbc = w[:, 5128:6664]
    dt = w[:, 6664:6680]
    main = jnp.concatenate([q, k, v, ga, z, xbc], axis=1).astype(BF16)
    small = jnp.zeros((D_MODEL, 3 * LANE), F32).at[:, 0:8].set(fl)
    small = small.at[:, LANE:LANE + 8].set(dt[:, 0:8]).at[:, 2 * LANE:2 * LANE + 8].set(dt[:, 8:16])
    small_t = jnp.zeros((32, D_MODEL), F32).at[0:8].set(fl.T).at[8:24].set(dt.T)
    return main, small.astype(BF16), small_t.astype(BF16)


def _even_layer(h, u, w_in, w_out, f_bias, conv_w, conv_b, dt_bias, a_log, d_skip, ssd_nw,
                next_norm_w, final, bsz, seq):
    w_main, w_small, w_small_t = _even_weights(w_in)
    big = _matmul(u, w_main, BF16, tm=1024, tn=512, name="even_in_proj")
    small, small_t = _small_proj(u, w_small, w_small_t)
    c_col, c_row = _fox_gate_cumsum(small, small_t, f_bias, bsz, seq)
    o_a = _fox_attention(big, c_col, c_row, bsz, seq)
    y = _ssd(big, small, small_t, conv_w, conv_b, dt_bias, a_log, d_skip, ssd_nw, bsz, seq)
    return _out_proj([o_a, y], w_out.astype(BF16), h, next_norm_w, final)


def _odd_layer(h, u, w_in, w_out, lb_logits, hgrn_nw, layer, next_norm_w, final, bsz, seq):
    w = w_in.astype(F32)
    w_main = jnp.concatenate([w[:, 0:2048], w[:, 4096:6144], w[:, 6144:8192]], axis=1).astype(BF16)
    w_f = w[:, 2048:4096].astype(BF16)
    big = _matmul(u, w_main, BF16, tm=1024, tn=512, name="odd_in_proj")
    fgate = _matmul(u, w_f, F32, tm=1024, tn=512, name="odd_gate_proj")
    o = _hgrn(big, fgate, lb_logits, hgrn_nw, layer, bsz, seq)
    return _out_proj([o], w_out.astype(BF16), h, next_norm_w, final)


def kernel(x, norm_w, final_norm_w, even_w_in, even_w_out, fox_f_bias, ssd_conv_w, ssd_conv_b, ssd_dt_bias,
           ssd_A_log, ssd_D, ssd_norm_w, odd_w_in, odd_w_out, hgrn_lb_logits, hgrn_norm_w):
    bsz, seq, d = x.shape
    depth = norm_w.shape[0]
    h = x.reshape(bsz * seq, d).astype(F32)
    u = _rmsnorm(h, norm_w[0].astype(F32))
    out = None
    for layer in range(depth):
        final = layer == depth - 1
        next_w = (final_norm_w if final else norm_w[layer + 1]).astype(F32)
        if layer % 2 == 0:
            e = layer // 2
            res = _even_layer(h, u, even_w_in[e], even_w_out[e], fox_f_bias[e], ssd_conv_w[e], ssd_conv_b[e],
                              ssd_dt_bias[e], ssd_A_log[e], ssd_D[e], ssd_norm_w[e], next_w, final, bsz, seq)
        else:
            o = layer // 2
            res = _odd_layer(h, u, odd_w_in[o], odd_w_out[o], hgrn_lb_logits, hgrn_norm_w[o], o,
                             next_w, final, bsz, seq)
        if final:
            out = res
        else:
            h, u = res
    return out.reshape(bsz, seq, d).astype(x.dtype)
```

```python
import functools

import jax
import jax.numpy as jnp
from jax import lax
from jax.experimental import pallas as pl
from jax.experimental.pallas import tpu as pltpu

F32 = jnp.float32
BF16 = jnp.bfloat16

D_MODEL = 1024
D_MIX = 2048
NORM_EPS = 1e-5

FOX_WIDTH = 1024
FOX_HEAD_DIM = 128
FOX_HEADS = 8

SSD_WIDTH = 1024
SSD_HEAD_DIM = 64
SSD_HEADS = 16
SSD_GROUPS = 2
SSD_HPG = 8
SSD_STATE = 128
SSD_CONV = 4
SSD_CHUNK = 128
SSD_GW = SSD_WIDTH // SSD_GROUPS

HGRN_WIDTH = 2048
HGRN_DK = 128
HGRN_HEADS = 16
HGRN_CHUNK = 64
HGRN_SUB = 8

LOG2E = 1.4426950408889634
LANE = 128
VMEM_LIMIT = 48 * 1024 * 1024

EV_Q, EV_K, EV_V, EV_GA, EV_Z, EV_X = 0, 1024, 2048, 3072, 4096, 5120
EV_B = EV_X + SSD_WIDTH
EV_C = EV_B + SSD_GROUPS * SSD_STATE
EV_COLS = EV_C + SSD_GROUPS * SSD_STATE
OD_Q, OD_I, OD_G = 0, 2048, 4096
OD_COLS = 6144


def _cparams(*sem):
    return pltpu.CompilerParams(dimension_semantics=sem, vmem_limit_bytes=VMEM_LIMIT)


def _dot(a, b):
    return jnp.dot(a, b, preferred_element_type=F32)


def _dot_nt(a, b):
    return lax.dot_general(a, b, (((1,), (1,)), ((), ())), preferred_element_type=F32)


def _dot_tn(a, b):
    return lax.dot_general(a, b, (((0,), (0,)), ((), ())), preferred_element_type=F32)


def _split3(x):
    hi = x.astype(BF16)
    r1 = x - hi.astype(F32)
    mid = r1.astype(BF16)
    lo = (r1 - mid.astype(F32)).astype(BF16)
    return hi, mid, lo


def _dot01_l(m01, x):
    hi, mid, lo = _split3(x)
    return _dot(m01, hi) + _dot(m01, mid) + _dot(m01, lo)


def _dot01_r(x, m01):
    hi, mid, lo = _split3(x)
    return _dot(hi, m01) + _dot(mid, m01) + _dot(lo, m01)


def _log_sigmoid(x):
    return jnp.minimum(x, 0.0) - jnp.log(1.0 + jnp.exp(-jnp.abs(x)))


def _softplus(x):
    return jnp.maximum(x, 0.0) + jnp.log(1.0 + jnp.exp(-jnp.abs(x)))


def _sigmoid(x):
    return 1.0 / (1.0 + jnp.exp(-x))


def _silu(x):
    return x * _sigmoid(x)


def _iota2(shape, dim):
    return lax.broadcasted_iota(jnp.int32, shape, dim)


def _rms_kernel(x_ref, w_ref, o_ref):
    x = x_ref[...]
    ms = jnp.mean(x * x, axis=-1, keepdims=True)
    o_ref[...] = (x * lax.rsqrt(ms + NORM_EPS) * w_ref[...]).astype(o_ref.dtype)


def _rmsnorm(x, w, tm=512):
    n, d = x.shape
    return pl.pallas_call(
        _rms_kernel,
        out_shape=jax.ShapeDtypeStruct((n, d), BF16),
        grid=(n // tm,),
        in_specs=[pl.BlockSpec((tm, d), lambda i: (i, 0)),
                  pl.BlockSpec((1, d), lambda i: (0, 0))],
        out_specs=pl.BlockSpec((tm, d), lambda i: (i, 0)),
        compiler_params=_cparams("parallel"),
        name="rmsnorm",
    )(x, w.reshape(1, d))


def _mm_kernel(a_ref, w_ref, o_ref):
    o_ref[...] = _dot(a_ref[...], w_ref[...]).astype(o_ref.dtype)


def _matmul(a, w, out_dtype, tm, tn, name):
    n, k = a.shape
    c = w.shape[1]
    return pl.pallas_call(
        _mm_kernel,
        out_shape=jax.ShapeDtypeStruct((n, c), out_dtype),
        grid=(n // tm, c // tn),
        in_specs=[pl.BlockSpec((tm, k), lambda i, j: (i, 0)),
                  pl.BlockSpec((k, tn), lambda i, j: (0, j))],
        out_specs=pl.BlockSpec((tm, tn), lambda i, j: (i, j)),
        compiler_params=_cparams("parallel", "arbitrary"),
        name=name,
    )(a, w)


def _small_proj_kernel(a_ref, w_ref, wt_ref, o_ref, ot_ref):
    a = a_ref[...]
    o_ref[...] = _dot(a, w_ref[...])
    ot_ref[...] = _dot_nt(wt_ref[...], a)


def _small_proj(a, w, wt, tm=512):
    n, k = a.shape
    c = w.shape[1]
    return pl.pallas_call(
        _small_proj_kernel,
        out_shape=(jax.ShapeDtypeStruct((n, c), F32), jax.ShapeDtypeStruct((32, n), F32)),
        grid=(n // tm,),
        in_specs=[pl.BlockSpec((tm, k), lambda i: (i, 0)),
                  pl.BlockSpec((k, c), lambda i: (0, 0)),
                  pl.BlockSpec((32, k), lambda i: (0, 0))],
        out_specs=(pl.BlockSpec((tm, c), lambda i: (i, 0)),
                   pl.BlockSpec((32, tm), lambda i: (0, i))),
        compiler_params=_cparams("parallel"),
        name="gate_proj",
    )(a, w, wt)


def _out_kernel(*refs, n_in, final):
    a_refs = refs[:n_in]
    w_ref, h_ref, nw_ref = refs[n_in:n_in + 3]
    outs = refs[n_in + 3:]
    acc = h_ref[...]
    k0 = 0
    for a_ref in a_refs:
        kk = a_ref.shape[1]
        acc = acc + _dot(a_ref[...], w_ref[k0:k0 + kk, :])
        k0 += kk
    ms = jnp.mean(acc * acc, axis=-1, keepdims=True)
    normed = acc * lax.rsqrt(ms + NORM_EPS) * nw_ref[...]
    if final:
        outs[0][...] = normed
    else:
        outs[0][...] = acc
        outs[1][...] = normed.astype(BF16)


def _out_proj(acts, w, h, next_norm_w, final, tm=512):
    n, d = h.shape
    kt = w.shape[0]
    in_specs = [pl.BlockSpec((tm, a.shape[1]), lambda i: (i, 0)) for a in acts]
    in_specs += [pl.BlockSpec((kt, d), lambda i: (0, 0)),
                 pl.BlockSpec((tm, d), lambda i: (i, 0)),
                 pl.BlockSpec((1, d), lambda i: (0, 0))]
    row = pl.BlockSpec((tm, d), lambda i: (i, 0))
    if final:
        out_shape = jax.ShapeDtypeStruct((n, d), F32)
        out_specs = row
    else:
        out_shape = (jax.ShapeDtypeStruct((n, d), F32), jax.ShapeDtypeStruct((n, d), BF16))
        out_specs = (row, row)
    return pl.pallas_call(
        functools.partial(_out_kernel, n_in=len(acts), final=final),
        out_shape=out_shape,
        grid=(n // tm,),
        in_specs=in_specs,
        out_specs=out_specs,
        compiler_params=_cparams("parallel"),
        name="out_proj",
    )(*acts, w, h, next_norm_w.reshape(1, d))


def _foxgate_kernel(s_ref, st_ref, bc_ref, br_ref, cc_ref, cr_ref, carry_c, carry_r, *, tc):
    @pl.when(pl.program_id(1) == 0)
    def _():
        carry_c[...] = jnp.zeros_like(carry_c)
        carry_r[...] = jnp.zeros_like(carry_r)

    ri = _iota2((tc, tc), 0)
    ci = _iota2((tc, tc), 1)
    ltri = (ci <= ri).astype(BF16)
    utri = (ri <= ci).astype(BF16)

    lc = _log_sigmoid(s_ref[...] + bc_ref[...])
    cc = _dot01_l(ltri, lc) + carry_c[0:1, :]
    cc_ref[...] = cc
    carry_c[...] = jnp.broadcast_to(cc[tc - 1:tc, :], carry_c.shape)

    lr = _log_sigmoid(st_ref[0:8, :] + br_ref[...])
    cr = _dot01_r(lr, utri) + carry_r[:, 0:1]
    cr_ref[...] = cr
    carry_r[...] = jnp.broadcast_to(cr[:, tc - 1:tc], carry_r.shape)


def _fox_gate_cumsum(small, small_t, f_bias, bsz, seq, tc=512):
    n = small.shape[0]
    nt = seq // tc
    bias_c = jnp.zeros((1, LANE), F32).at[0, :FOX_HEADS].set(f_bias.astype(F32))
    bias_r = jnp.broadcast_to(f_bias.astype(F32)[:, None], (FOX_HEADS, tc))
    return pl.pallas_call(
        functools.partial(_foxgate_kernel, tc=tc),
        out_shape=(jax.ShapeDtypeStruct((n, LANE), F32), jax.ShapeDtypeStruct((FOX_HEADS, n), F32)),
        grid=(bsz, nt),
        in_specs=[pl.BlockSpec((tc, LANE), lambda b, j: (b * nt + j, 0)),
                  pl.BlockSpec((32, tc), lambda b, j: (0, b * nt + j)),
                  pl.BlockSpec((1, LANE), lambda b, j: (0, 0)),
                  pl.BlockSpec((FOX_HEADS, tc), lambda b, j: (0, 0))],
        out_specs=(pl.BlockSpec((tc, LANE), lambda b, j: (b * nt + j, 0)),
                   pl.BlockSpec((FOX_HEADS, tc), lambda b, j: (0, b * nt + j))),
        scratch_shapes=[pltpu.VMEM((8, LANE), F32), pltpu.VMEM((8, LANE), F32)],
        compiler_params=_cparams("parallel", "arbitrary"),
        name="fox_gate_cumsum",
    )(small, small_t, bias_c, bias_r)


def _fox_kernel(q_ref, k_ref, v_ref, g_ref, cc_ref, cr_ref, o_ref, *, tq, scale):
    h = pl.program_id(1)
    qi = pl.program_id(2)
    q = q_ref[...]
    lane = _iota2((tq, LANE), 1)
    cq = jnp.sum(jnp.where(lane == h, cc_ref[...], 0.0), axis=-1, keepdims=True)

    def block(kj, carry, diagonal):
        m, l, acc = carry
        start = pl.multiple_of(kj * tq, tq)
        k = k_ref[pl.ds(start, tq), :]
        v = v_ref[pl.ds(start, tq), :]
        ck = cr_ref[kj]
        s = _dot_nt(q, k) * scale + (cq - ck)
        if diagonal:
            s = jnp.where(_iota2((tq, tq), 0) >= _iota2((tq, tq), 1), s, -jnp.inf)
        m_new = jnp.maximum(m, jnp.max(s, axis=-1, keepdims=True))
        alpha = jnp.exp(m - m_new)
        p = jnp.exp(s - m_new)
        l = alpha * l + jnp.sum(p, axis=-1, keepdims=True)
        acc = alpha * acc + _dot(p.astype(BF16), v)
        return m_new, l, acc

    init = (jnp.full((tq, 1), -1e30, F32), jnp.zeros((tq, 1), F32), jnp.zeros((tq, FOX_HEAD_DIM), F32))
    carry = lax.fori_loop(0, qi, lambda kj, c: block(kj, c, False), init)
    _, l, acc = block(qi, carry, True)
    o_ref[...] = (acc / l * _silu(g_ref[...].astype(F32))).astype(o_ref.dtype)


def _fox_attention(big, c_col, c_row, bsz, seq, tq=512):
    n = big.shape[0]
    nq = seq // tq
    qb, kb, vb, gb = EV_Q // LANE, EV_K // LANE, EV_V // LANE, EV_GA // LANE
    return pl.pallas_call(
        functools.partial(_fox_kernel, tq=tq, scale=FOX_HEAD_DIM ** -0.5),
        out_shape=jax.ShapeDtypeStruct((n, FOX_WIDTH), BF16),
        grid=(bsz, FOX_HEADS, nq),
        in_specs=[pl.BlockSpec((tq, LANE), lambda b, h, i: (b * nq + i, qb + h)),
                  pl.BlockSpec((seq, LANE), lambda b, h, i: (b, kb + h)),
                  pl.BlockSpec((seq, LANE), lambda b, h, i: (b, vb + h)),
                  pl.BlockSpec((tq, LANE), lambda b, h, i: (b * nq + i, gb + h)),
                  pl.BlockSpec((tq, LANE), lambda b, h, i: (b * nq + i, 0)),
                  pl.BlockSpec((None, nq, 1, tq), lambda b, h, i: (h, b, 0, 0))],
        out_specs=pl.BlockSpec((tq, LANE), lambda b, h, i: (b * nq + i, h)),
        compiler_params=_cparams("parallel", "parallel", "arbitrary"),
        name="fox_attention",
    )(big, big, big, big, c_col, c_row.reshape(FOX_HEADS, n // tq, 1, tq))


def _ssd_kernel(x_ref, b_ref, c_ref, z_ref, dtc_ref, dtr_ref,
                wx_ref, wb_ref, wc_ref, bx_ref, bb_ref, bc_ref,
                dbc_ref, dbr_ref, alc_ref, alr_ref, e_ref, e128_ref, dsk_ref, nw_ref,
                o_ref, pad_ref, st_ref, y_ref):
    lc = SSD_CHUNK
    gw = SSD_GW
    cw = gw + 2 * SSD_STATE

    @pl.when(pl.program_id(2) == 0)
    def _():
        pad_ref[0:8, :] = jnp.zeros((8, cw), F32)
        st_ref[...] = jnp.zeros_like(st_ref)

    pad_ref[8:8 + lc, 0:gw] = x_ref[...].astype(F32)
    pad_ref[8:8 + lc, gw:gw + SSD_STATE] = b_ref[...].astype(F32)
    pad_ref[8:8 + lc, gw + SSD_STATE:cw] = c_ref[...].astype(F32)
    w_all = jnp.concatenate([wx_ref[...], wb_ref[...], wc_ref[...]], axis=1)
    b_all = jnp.concatenate([bx_ref[...], bb_ref[...], bc_ref[...]], axis=1)
    conv = b_all + w_all[0:1, :] * pad_ref[5:5 + lc, :]
    for kk in range(1, SSD_CONV):
        conv = conv + w_all[kk:kk + 1, :] * pad_ref[5 + kk:5 + kk + lc, :]
    pad_ref[0:8, :] = pad_ref[lc:lc + 8, :]
    conv = _silu(conv)
    xc = conv[:, 0:gw]
    bm = conv[:, gw:gw + SSD_STATE].astype(BF16)
    cm = conv[:, gw + SSD_STATE:cw].astype(BF16)

    ri = _iota2((lc, lc), 0)
    ci = _iota2((lc, lc), 1)
    causal = ri >= ci
    ltri = causal.astype(BF16)
    utri = (ri <= ci).astype(BF16)

    dt_c = _softplus(dtc_ref[...] + dbc_ref[...])
    a_c = dt_c * (-jnp.exp(alc_ref[...]))
    acum_c = _dot01_l(ltri, a_c)
    dt_r = _softplus(dtr_ref[...] + dbr_ref[...])
    a_r = dt_r * (-jnp.exp(alr_ref[...]))
    acum_r = _dot01_r(a_r, utri)

    e = e_ref[...]
    dt_e = _dot01_r(dt_c, e)
    acum_e = _dot01_r(acum_c, e)
    acol = _dot01_r(acum_c, e128_ref[...])
    alast_e = acum_e[lc - 1:lc, :]

    xd = xc * dt_e
    xd_b = xd.astype(BF16)
    cb = _dot_nt(cm, bm)
    for j in range(SSD_HPG):
        seg = acol[:, j * LANE:(j + 1) * LANE] - acum_r[j:j + 1, :]
        ldec = jnp.exp(jnp.where(causal, seg, -jnp.inf))
        mj = (cb * ldec).astype(BF16)
        y_ref[:, j * SSD_HEAD_DIM:(j + 1) * SSD_HEAD_DIM] = _dot(
            mj, xd_b[:, j * SSD_HEAD_DIM:(j + 1) * SSD_HEAD_DIM])

    st_prev = st_ref[...]
    y_off = _dot(cm, st_prev.astype(BF16)) * jnp.exp(acum_e)
    wgt = (xd * jnp.exp(alast_e - acum_e)).astype(BF16)
    st_ref[...] = st_prev * jnp.exp(alast_e) + _dot_tn(bm, wgt)

    y = y_ref[...] + y_off + dsk_ref[...] * xc
    y = y * _silu(z_ref[...].astype(F32))
    ms = jnp.mean(y * y, axis=-1, keepdims=True)
    o_ref[...] = (y * lax.rsqrt(ms + NORM_EPS) * nw_ref[...]).astype(o_ref.dtype)


def _ssd(big, small, small_t, conv_w, conv_b, dt_bias, a_log, d_skip, norm_w, bsz, seq):
    n = big.shape[0]
    lc = SSD_CHUNK
    nc = seq // lc
    gw = SSD_GW
    f32 = lambda a: a.astype(F32)
    def col_layout(vec):
        out = jnp.zeros((SSD_GROUPS, 1, LANE), F32)
        return out.at[:, 0, :SSD_HPG].set(f32(vec).reshape(SSD_GROUPS, SSD_HPG))
    def row_layout(vec):
        return jnp.broadcast_to(f32(vec).reshape(SSD_GROUPS, SSD_HPG, 1), (SSD_GROUPS, SSD_HPG, lc))
    expand = (jnp.arange(LANE)[:, None] == (jnp.arange(gw) // SSD_HEAD_DIM)[None, :]).astype(BF16)
    expand128 = (jnp.arange(LANE)[:, None] == (jnp.arange(SSD_HPG * LANE) // LANE)[None, :]).astype(BF16)
    xb = EV_X // gw
    bb = EV_B // SSD_STATE
    cb = EV_C // SSD_STATE
    zb = EV_Z // gw
    row = lambda b, g, c: b * nc + c
    cw_x = lambda b, g, c: (0, g)
    cw_b = lambda b, g, c: (0, SSD_WIDTH // SSD_STATE + g)
    cw_c = lambda b, g, c: (0, SSD_WIDTH // SSD_STATE + SSD_GROUPS + g)
    gsel3 = lambda b, g, c: (g, 0, 0)
    return pl.pallas_call(
        _ssd_kernel,
        out_shape=jax.ShapeDtypeStruct((n, SSD_WIDTH), BF16),
        grid=(bsz, SSD_GROUPS, nc),
        in_specs=[
            pl.BlockSpec((lc, gw), lambda b, g, c: (row(b, g, c), xb + g)),
            pl.BlockSpec((lc, SSD_STATE), lambda b, g, c: (row(b, g, c), bb + g)),
            pl.BlockSpec((lc, SSD_STATE), lambda b, g, c: (row(b, g, c), cb + g)),
            pl.BlockSpec((lc, gw), lambda b, g, c: (row(b, g, c), zb + g)),
            pl.BlockSpec((lc, LANE), lambda b, g, c: (row(b, g, c), 1 + g)),
            pl.BlockSpec((8, lc), lambda b, g, c: (1 + g, row(b, g, c))),
            pl.BlockSpec((SSD_CONV, gw), cw_x),
            pl.BlockSpec((SSD_CONV, SSD_STATE), cw_b),
            pl.BlockSpec((SSD_CONV, SSD_STATE), cw_c),
            pl.BlockSpec((1, gw), cw_x),
            pl.BlockSpec((1, SSD_STATE), cw_b),
            pl.BlockSpec((1, SSD_STATE), cw_c),
            pl.BlockSpec((None, 1, LANE), gsel3),
            pl.BlockSpec((None, SSD_HPG, lc), gsel3),
            pl.BlockSpec((None, 1, LANE), gsel3),
            pl.BlockSpec((None, SSD_HPG, lc), gsel3),
            pl.BlockSpec((LANE, gw), lambda b, g, c: (0, 0)),
            pl.BlockSpec((LANE, SSD_HPG * LANE), lambda b, g, c: (0, 0)),
            pl.BlockSpec((1, gw), lambda b, g, c: (0, g)),
            pl.BlockSpec((1, gw), lambda b, g, c: (0, g)),
        ],
        out_specs=pl.BlockSpec((lc, gw), lambda b, g, c: (row(b, g, c), g)),
        scratch_shapes=[pltpu.VMEM((lc + 8, gw + 2 * SSD_STATE), F32),
                        pltpu.VMEM((SSD_STATE, gw), F32),
                        pltpu.VMEM((lc, gw), F32)],
        compiler_params=_cparams("parallel", "parallel", "arbitrary"),
        name="ssd_scan",
    )(big, big, big, big, small, small_t,
      f32(conv_w), f32(conv_w), f32(conv_w),
      f32(conv_b).reshape(1, -1), f32(conv_b).reshape(1, -1), f32(conv_b).reshape(1, -1),
      col_layout(dt_bias), row_layout(dt_bias), col_layout(a_log), row_layout(a_log),
      expand, expand128,
      jnp.repeat(f32(d_skip), SSD_HEAD_DIM).reshape(1, SSD_WIDTH),
      f32(norm_w).reshape(1, SSD_WIDTH))


def _hgrn_kernel(q_ref, i_ref, gt_ref, f_ref, lbl_ref, nw_ref, o_ref, st_ref, row_ref, *, tb, layer):
    lc = HGRN_CHUNK
    sub = HGRN_SUB
    nsub = lc // sub
    dk = HGRN_DK

    @pl.when(pl.program_id(2) == 0)
    def _():
        st_ref[...] = jnp.zeros_like(st_ref)

    lbl = lbl_ref[...].astype(F32)
    ex = jnp.exp(lbl - jnp.max(lbl, axis=0, keepdims=True))
    pr = ex / jnp.sum(ex, axis=0, keepdims=True)
    lb = jnp.zeros((1, dk), F32)
    for r in range(1, layer + 1):
        lb = lb + pr[r:r + 1, :]
    log_lb = jnp.log(lb)
    log_1mlb = jnp.log(1.0 - lb)

    ri = _iota2((lc, lc), 0)
    ci = _iota2((lc, lc), 1)
    sub_shift = sub.bit_length() - 1
    sub_start = jnp.left_shift(jnp.right_shift(ri, sub_shift), sub_shift)
    l_within = ((ci >= sub_start) & (ci <= ri)).astype(BF16)
    l_before = (ci < sub_start).astype(BF16)
    l_cum = jnp.concatenate([l_within, l_before], axis=0)
    ncat = sub * (nsub * (nsub - 1) // 2)
    seg_of_col = jnp.zeros((lc, ncat), jnp.int32)
    off = 0
    for i in range(1, nsub):
        cols = _iota2((lc, ncat), 1)
        seg_of_col = jnp.where((cols >= off) & (cols < off + sub * i), i, seg_of_col)
        off += sub * i
    cat_mask = jnp.right_shift(_iota2((lc, ncat), 0), sub_shift) == seg_of_col
    ones_b = jnp.ones((dk, dk), BF16)
    tsub = _iota2((sub, dk), 0)

    nchunk = tb // lc
    rows = lambda a, c: a[c * lc:(c + 1) * lc]

    qf = _silu(q_ref[...].astype(F32))
    ff = f_ref[...]
    v = i_ref[...].astype(F32)
    v_b = i_ref[...]
    ea = jnp.exp(-jnp.abs(ff))
    log_sig = jnp.minimum(ff, 0.0) - jnp.log(1.0 + ea)
    sig_neg = jnp.where(ff >= 0.0, ea, 1.0) / (1.0 + ea)
    t2 = log_1mlb + log_sig
    mx = jnp.maximum(log_lb, t2)
    g = mx + jnp.log(1.0 + jnp.exp(jnp.minimum(log_lb, t2) - mx))
    kin = (1.0 - lb) * sig_neg

    g2 = g * LOG2E
    cums = _dot01_l(l_cum, jnp.concatenate([rows(g2, c) for c in range(nchunk)], axis=1))
    bw = jnp.concatenate([cums[0:lc, c * dk:(c + 1) * dk] for c in range(nchunk)], axis=0)
    beta = jnp.concatenate([cums[lc:2 * lc, c * dk:(c + 1) * dk] for c in range(nchunk)], axis=0)
    bfull = bw + beta
    qd = qf * jnp.exp2(bw)
    qd_b = qd.astype(BF16)
    qs_b = (qd * jnp.exp2(beta)).astype(BF16)
    blast = [bfull[(c + 1) * lc - 1:(c + 1) * lc, :] for c in range(nchunk)]
    blast_rows = jnp.concatenate([jnp.broadcast_to(b, (lc, dk)) for b in blast], axis=0)
    kd_b = (kin * jnp.exp2(blast_rows - bfull)).astype(BF16)

    upd = [_dot_tn(rows(v_b, c), rows(kd_b, c)) for c in range(nchunk)]
    kcats = []
    vcats = []
    for c in range(nchunk):
        kin_c, beta_c, bf_c, vb_c = rows(kin, c), rows(beta, c), rows(bfull, c), rows(v_b, c)
        kcat = []
        vcat = []
        for i in range(1, nsub):
            beta_i = beta_c[i * sub:i * sub + 1, :]
            kcat.append(kin_c[0:i * sub, :] * jnp.exp2(beta_i - bf_c[0:i * sub, :]))
            vcat.append(vb_c[0:i * sub, :])
        kcats.append(jnp.concatenate(kcat, axis=0).astype(BF16))
        vcats.append(jnp.concatenate(vcat, axis=0))
    a_raw = [_dot_nt(rows(qd_b, c), kcats[c]) for c in range(nchunk)]
    o_cat = [_dot(jnp.where(cat_mask, a_raw[c], 0.0).astype(BF16), vcats[c]) for c in range(nchunk)]

    nblk = tb // sub
    row_ref[0] = bw
    row_ref[1] = kin
    row_ref[2] = v
    prods = []
    for blk in range(nblk):
        sl = slice(blk * sub, (blk + 1) * sub)
        q_s, b_s = qf[sl], bw[sl]
        for s in range(sub):
            r = blk * sub + s
            diff = b_s - row_ref[0, r:r + 1, :]
            if s > 0:
                diff = jnp.where(tsub >= s, diff, -jnp.inf)
            prods.append(q_s * row_ref[1, r:r + 1, :] * jnp.exp2(diff))
    rs = _dot(jnp.concatenate(prods, axis=0).astype(BF16), ones_b)
    o_diag = []
    for blk in range(nblk):
        base = blk * sub * sub
        r = blk * sub
        od = rs[base:base + sub, :] * row_ref[2, r:r + 1, :]
        for s in range(1, sub):
            od = od + rs[base + s * sub:base + (s + 1) * sub, :] * row_ref[2, r + s:r + s + 1, :]
        o_diag.append(od)

    st = st_ref[...]
    o_state = []
    for c in range(nchunk):
        o_state.append(_dot_nt(rows(qs_b, c), st.astype(BF16)))
        st = st * jnp.exp2(blast[c]) + upd[c]
    st_ref[...] = st

    o = jnp.concatenate(o_state, axis=0) + jnp.concatenate(o_cat, axis=0) + jnp.concatenate(o_diag, axis=0)
    ms = jnp.mean(o * o, axis=-1, keepdims=True)
    o = o * lax.rsqrt(ms + NORM_EPS) * nw_ref[...]
    o = o * _silu(gt_ref[...].astype(F32))
    o_ref[...] = o.astype(o_ref.dtype)


def _hgrn(big, fgate, lb_logits, norm_w, layer, bsz, seq, tb=512):
    n = big.shape[0]
    nt = seq // tb
    nl = lb_logits.shape[0]
    qb, ib, gb = OD_Q // LANE, OD_I // LANE, OD_G // LANE
    row = lambda b, h, t: b * nt + t
    return pl.pallas_call(
        functools.partial(_hgrn_kernel, tb=tb, layer=layer),
        out_shape=jax.ShapeDtypeStruct((n, HGRN_WIDTH), BF16),
        grid=(bsz, HGRN_HEADS, nt),
        in_specs=[pl.BlockSpec((tb, LANE), lambda b, h, t: (row(b, h, t), qb + h)),
                  pl.BlockSpec((tb, LANE), lambda b, h, t: (row(b, h, t), ib + h)),
                  pl.BlockSpec((tb, LANE), lambda b, h, t: (row(b, h, t), gb + h)),
                  pl.BlockSpec((tb, LANE), lambda b, h, t: (row(b, h, t), h)),
                  pl.BlockSpec((nl, LANE), lambda b, h, t: (0, h)),
                  pl.BlockSpec((1, LANE), lambda b, h, t: (0, 0))],
        out_specs=pl.BlockSpec((tb, LANE), lambda b, h, t: (row(b, h, t), h)),
        scratch_shapes=[pltpu.VMEM((HGRN_DK, HGRN_DK), F32), pltpu.VMEM((3, tb, LANE), F32)],
        compiler_params=_cparams("parallel", "parallel", "arbitrary"),
        name="hgrn2_scan",
    )(big, big, big, fgate, lb_logits.astype(F32), norm_w.astype(F32).reshape(1, LANE))


def _even_weights(w_in):
    w = w_in.astype(F32)
    q, k, v, ga = (w[:, i * 1024:(i + 1) * 1024] for i in range(4))
    fl = w[:, 4096:4104]
    z = w[:, 4104:5128]
    xbc = w[:, 5128:6664]
    dt = w[:, 6664:6680]
    main = jnp.concatenate([q, k, v, ga, z, xbc], axis=1).astype(BF16)
    small = jnp.zeros((D_MODEL, 3 * LANE), F32).at[:, 0:8].set(fl)
    small = small.at[:, LANE:LANE + 8].set(dt[:, 0:8]).at[:, 2 * LANE:2 * LANE + 8].set(dt[:, 8:16])
    small_t = jnp.zeros((32, D_MODEL), F32).at[0:8].set(fl.T).at[8:24].set(dt.T)
    return main, small.astype(BF16), small_t.astype(BF16)


def _even_layer(h, u, w_in, w_out, f_bias, conv_w, conv_b, dt_bias, a_log, d_skip, ssd_nw,
                next_norm_w, final, bsz, seq):
    w_main, w_small, w_small_t = _even_weights(w_in)
    big = _matmul(u, w_main, BF16, tm=1024, tn=512, name="even_in_proj")
    small, small_t = _small_proj(u, w_small, w_small_t)
    c_col, c_row = _fox_gate_cumsum(small, small_t, f_bias, bsz, seq)
    o_a = _fox_attention(big, c_col, c_row, bsz, seq)
    y = _ssd(big, small, small_t, conv_w, conv_b, dt_bias, a_log, d_skip, ssd_nw, bsz, seq)
    return _out_proj([o_a, y], w_out.astype(BF16), h, next_norm_w, final)


def _odd_layer(h, u, w_in, w_out, lb_logits, hgrn_nw, layer, next_norm_w, final, bsz, seq):
    w = w_in.astype(F32)
    w_main = jnp.concatenate([w[:, 0:2048], w[:, 4096:6144], w[:, 6144:8192]], axis=1).astype(BF16)
    w_f = w[:, 2048:4096].astype(BF16)
    big = _matmul(u, w_main, BF16, tm=1024, tn=512, name="odd_in_proj")
    fgate = _matmul(u, w_f, F32, tm=1024, tn=512, name="odd_gate_proj")
    o = _hgrn(big, fgate, lb_logits, hgrn_nw, layer, bsz, seq)
    return _out_proj([o], w_out.astype(BF16), h, next_norm_w, final)


def kernel(x, norm_w, final_norm_w, even_w_in, even_w_out, fox_f_bias, ssd_conv_w, ssd_conv_b, ssd_dt_bias,
           ssd_A_log, ssd_D, ssd_norm_w, odd_w_in, odd_w_out, hgrn_lb_logits, hgrn_norm_w):
    bsz, seq, d = x.shape
    depth = norm_w.shape[0]
    h = x.reshape(bsz * seq, d).astype(F32)
    u = _rmsnorm(h, norm_w[0].astype(F32))
    out = None
    for layer in range(depth):
        final = layer == depth - 1
        next_w = (final_norm_w if final else norm_w[layer + 1]).astype(F32)
        if layer % 2 == 0:
            e = layer // 2
            res = _even_layer(h, u, even_w_in[e], even_w_out[e], fox_f_bias[e], ssd_conv_w[e], ssd_conv_b[e],
                              ssd_dt_bias[e], ssd_A_log[e], ssd_D[e], ssd_norm_w[e], next_w, final, bsz, seq)
        else:
            o = layer // 2
            res = _odd_layer(h, u, odd_w_in[o], odd_w_out[o], hgrn_lb_logits, hgrn_norm_w[o], o,
                             next_w, final, bsz, seq)
        if final:
            out = res
        else:
            h, u = res
    return out.reshape(bsz, seq, d).astype(x.dtype)
```

```python
import functools

import jax
import jax.numpy as jnp
from jax import lax
from jax.experimental import pallas as pl
from jax.experimental.pallas import tpu as pltpu

F32 = jnp.float32
BF16 = jnp.bfloat16

D_MODEL = 1024
D_MIX = 2048
NORM_EPS = 1e-5

FOX_WIDTH = 1024
FOX_HEAD_DIM = 128
FOX_HEADS = 8

SSD_WIDTH = 1024
SSD_HEAD_DIM = 64
SSD_HEADS = 16
SSD_GROUPS = 2
SSD_HPG = 8
SSD_STATE = 128
SSD_CONV = 4
SSD_CHUNK = 128
SSD_GW = SSD_WIDTH // SSD_GROUPS

HGRN_WIDTH = 2048
HGRN_DK = 128
HGRN_HEADS = 16
HGRN_CHUNK = 64
HGRN_SUB = 8

LOG2E = 1.4426950408889634
LANE = 128
VMEM_LIMIT = 48 * 1024 * 1024

EV_Q, EV_K, EV_V, EV_GA, EV_Z, EV_X = 0, 1024, 2048, 3072, 4096, 5120
EV_B = EV_X + SSD_WIDTH
EV_C = EV_B + SSD_GROUPS * SSD_STATE
EV_COLS = EV_C + SSD_GROUPS * SSD_STATE
OD_Q, OD_I, OD_G = 0, 2048, 4096
OD_COLS = 6144


def _cparams(*sem):
    return pltpu.CompilerParams(dimension_semantics=sem, vmem_limit_bytes=VMEM_LIMIT)


def _dot(a, b):
    return jnp.dot(a, b, preferred_element_type=F32)


def _dot_nt(a, b):
    return lax.dot_general(a, b, (((1,), (1,)), ((), ())), preferred_element_type=F32)


def _dot_tn(a, b):
    return lax.dot_general(a, b, (((0,), (0,)), ((), ())), preferred_element_type=F32)


def _split3(x):
    hi = x.astype(BF16)
    r1 = x - hi.astype(F32)
    mid = r1.astype(BF16)
    lo = (r1 - mid.astype(F32)).astype(BF16)
    return hi, mid, lo


def _dot01_l(m01, x):
    hi, mid, lo = _split3(x)
    return _dot(m01, hi) + _dot(m01, mid) + _dot(m01, lo)


def _dot01_r(x, m01):
    hi, mid, lo = _split3(x)
    return _dot(hi, m01) + _dot(mid, m01) + _dot(lo, m01)


def _log_sigmoid(x):
    return jnp.minimum(x, 0.0) - jnp.log(1.0 + jnp.exp(-jnp.abs(x)))


def _softplus(x):
    return jnp.maximum(x, 0.0) + jnp.log(1.0 + jnp.exp(-jnp.abs(x)))


def _sigmoid(x):
    return 1.0 / (1.0 + jnp.exp(-x))


def _silu(x):
    return x * _sigmoid(x)


def _iota2(shape, dim):
    return lax.broadcasted_iota(jnp.int32, shape, dim)


def _rms_kernel(x_ref, w_ref, o_ref):
    x = x_ref[...]
    ms = jnp.mean(x * x, axis=-1, keepdims=True)
    o_ref[...] = (x * lax.rsqrt(ms + NORM_EPS) * w_ref[...]).astype(o_ref.dtype)


def _rmsnorm(x, w, tm=512):
    n, d = x.shape
    return pl.pallas_call(
        _rms_kernel,
        out_shape=jax.ShapeDtypeStruct((n, d), BF16),
        grid=(n // tm,),
        in_specs=[pl.BlockSpec((tm, d), lambda i: (i, 0)),
                  pl.BlockSpec((1, d), lambda i: (0, 0))],
        out_specs=pl.BlockSpec((tm, d), lambda i: (i, 0)),
        compiler_params=_cparams("parallel"),
        name="rmsnorm",
    )(x, w.reshape(1, d))


def _mm_kernel(a_ref, w_ref, o_ref):
    o_ref[...] = _dot(a_ref[...], w_ref[...]).astype(o_ref.dtype)


def _matmul(a, w, out_dtype, tm, tn, name):
    n, k = a.shape
    c = w.shape[1]
    return pl.pallas_call(
        _mm_kernel,
        out_shape=jax.ShapeDtypeStruct((n, c), out_dtype),
        grid=(n // tm, c // tn),
        in_specs=[pl.BlockSpec((tm, k), lambda i, j: (i, 0)),
                  pl.BlockSpec((k, tn), lambda i, j: (0, j))],
        out_specs=pl.BlockSpec((tm, tn), lambda i, j: (i, j)),
        compiler_params=_cparams("parallel", "arbitrary"),
        name=name,
    )(a, w)


def _small_proj_kernel(a_ref, w_ref, wt_ref, o_ref, ot_ref):
    a = a_ref[...]
    o_ref[...] = _dot(a, w_ref[...])
    ot_ref[...] = _dot_nt(wt_ref[...], a)


def _small_proj(a, w, wt, tm=512):
    n, k = a.shape
    c = w.shape[1]
    return pl.pallas_call(
        _small_proj_kernel,
        out_shape=(jax.ShapeDtypeStruct((n, c), F32), jax.ShapeDtypeStruct((32, n), F32)),
        grid=(n // tm,),
        in_specs=[pl.BlockSpec((tm, k), lambda i: (i, 0)),
                  pl.BlockSpec((k, c), lambda i: (0, 0)),
                  pl.BlockSpec((32, k), lambda i: (0, 0))],
        out_specs=(pl.BlockSpec((tm, c), lambda i: (i, 0)),
                   pl.BlockSpec((32, tm), lambda i: (0, i))),
        compiler_params=_cparams("parallel"),
        name="gate_proj",
    )(a, w, wt)


def _out_kernel(*refs, n_in, final):
    a_refs = refs[:n_in]
    w_ref, h_ref, nw_ref = refs[n_in:n_in + 3]
    outs = refs[n_in + 3:]
    acc = h_ref[...]
    k0 = 0
    for a_ref in a_refs:
        kk = a_ref.shape[1]
        acc = acc + _dot(a_ref[...], w_ref[k0:k0 + kk, :])
        k0 += kk
    ms = jnp.mean(acc * acc, axis=-1, keepdims=True)
    normed = acc * lax.rsqrt(ms + NORM_EPS) * nw_ref[...]
    if final:
        outs[0][...] = normed
    else:
        outs[0][...] = acc
        outs[1][...] = normed.astype(BF16)


def _out_proj(acts, w, h, next_norm_w, final, tm=512):
    n, d = h.shape
    kt = w.shape[0]
    in_specs = [pl.BlockSpec((tm, a.shape[1]), lambda i: (i, 0)) for a in acts]
    in_specs += [pl.BlockSpec((kt, d), lambda i: (0, 0)),
                 pl.BlockSpec((tm, d), lambda i: (i, 0)),
                 pl.BlockSpec((1, d), lambda i: (0, 0))]
    row = pl.BlockSpec((tm, d), lambda i: (i, 0))
    if final:
        out_shape = jax.ShapeDtypeStruct((n, d), F32)
        out_specs = row
    else:
        out_shape = (jax.ShapeDtypeStruct((n, d), F32), jax.ShapeDtypeStruct((n, d), BF16))
        out_specs = (row, row)
    return pl.pallas_call(
        functools.partial(_out_kernel, n_in=len(acts), final=final),
        out_shape=out_shape,
        grid=(n // tm,),
        in_specs=in_specs,
        out_specs=out_specs,
        compiler_params=_cparams("parallel"),
        name="out_proj",
    )(*acts, w, h, next_norm_w.reshape(1, d))


FOX_AUG = 3


def _foxgate_kernel(s_ref, bc_ref, pk_ref, pq_ref, ka_ref, qa_ref, carry_c, *, tc):
    @pl.when(pl.program_id(1) == 0)
    def _():
        carry_c[...] = jnp.zeros_like(carry_c)

    ltri = (_iota2((tc, tc), 1) <= _iota2((tc, tc), 0)).astype(BF16)
    lc = _log_sigmoid(s_ref[...] + bc_ref[...])
    cc = _dot01_l(ltri, lc) + carry_c[0:1, :]
    carry_c[...] = jnp.broadcast_to(cc[tc - 1:tc, :], carry_c.shape)

    pieces = jnp.concatenate(_split3(cc * LOG2E), axis=1)
    lane = jnp.bitwise_and(_iota2((tc, FOX_HEADS * LANE), 1), LANE - 1)
    ones_k = jnp.where((lane >= FOX_AUG) & (lane < 2 * FOX_AUG), 1.0, 0.0)
    ones_q = jnp.where(lane < FOX_AUG, 1.0, 0.0)
    ka_ref[...] = (ones_k - _dot(pieces, pk_ref[...])).astype(BF16)
    qa_ref[...] = (ones_q + _dot(pieces, pq_ref[...])).astype(BF16)


def _fox_gate_cumsum(small, f_bias, bsz, seq, tc=512):
    n = small.shape[0]
    nt = seq // tc
    bias_c = jnp.zeros((1, LANE), F32).at[0, :FOX_HEADS].set(f_bias.astype(F32))
    src = jnp.arange(FOX_AUG * LANE)
    dst = jnp.arange(FOX_HEADS * LANE)
    same_head = (src[:, None] % LANE) == (dst[None, :] // LANE)
    piece = src[:, None] // LANE
    place_k = (same_head & (dst[None, :] % LANE == piece)).astype(BF16)
    place_q = (same_head & (dst[None, :] % LANE == piece + FOX_AUG)).astype(BF16)
    wide = FOX_HEADS * LANE
    return pl.pallas_call(
        functools.partial(_foxgate_kernel, tc=tc),
        out_shape=(jax.ShapeDtypeStruct((n, wide), BF16), jax.ShapeDtypeStruct((n, wide), BF16)),
        grid=(bsz, nt),
        in_specs=[pl.BlockSpec((tc, LANE), lambda b, j: (b * nt + j, 0)),
                  pl.BlockSpec((1, LANE), lambda b, j: (0, 0)),
                  pl.BlockSpec((FOX_AUG * LANE, wide), lambda b, j: (0, 0)),
                  pl.BlockSpec((FOX_AUG * LANE, wide), lambda b, j: (0, 0))],
        out_specs=(pl.BlockSpec((tc, wide), lambda b, j: (b * nt + j, 0)),
                   pl.BlockSpec((tc, wide), lambda b, j: (b * nt + j, 0))),
        scratch_shapes=[pltpu.VMEM((8, LANE), F32)],
        compiler_params=_cparams("parallel", "arbitrary"),
        name="fox_gate_cumsum",
    )(small, bias_c, place_k, place_q)


def _fox_kernel(q_ref, qa_ref, k_ref, ka_ref, v_ref, g_ref, o_ref, *, tq, tk, nsplit):
    qi = pl.program_id(2)
    hd = FOX_HEAD_DIM
    th = tq // nsplit
    qq = [jnp.concatenate([q_ref[h * th:(h + 1) * th, :], qa_ref[h * th:(h + 1) * th, :]], axis=1)
          for h in range(nsplit)]
    ones_v = jnp.ones((tk, hd), BF16)
    nblk = (qi + 1) * (tq // tk)
    row0 = qi * tq + _iota2((th, tk), 0)
    col0 = _iota2((th, tk), 1)

    def logits(kj):
        start = pl.multiple_of(kj * tk, tk)
        kk = jnp.concatenate([k_ref[pl.ds(start, tk), :], ka_ref[pl.ds(start, tk), :]], axis=1)
        col = col0 + kj * tk
        return tuple(jnp.where(row0 + h * th >= col, _dot_nt(qq[h], kk), -jnp.inf) for h in range(nsplit))

    def update(kj, ss, ms, accs):
        start = pl.multiple_of(kj * tk, tk)
        vv = jnp.concatenate([v_ref[pl.ds(start, tk), :], ones_v], axis=1)
        ms_new, accs_new = [], []
        for s, m, acc in zip(ss, ms, accs):
            m_new = jnp.maximum(m, jnp.max(s, axis=-1, keepdims=True))
            p = jnp.exp2(s - m_new)
            accs_new.append(jnp.exp2(m - m_new) * acc + _dot(p.astype(BF16), vv))
            ms_new.append(m_new)
        return tuple(ms_new), tuple(accs_new)

    def body(i, carry):
        ss0 = logits(2 * i)
        ss1 = logits(2 * i + 1)
        carry = update(2 * i, ss0, *carry)
        return update(2 * i + 1, ss1, *carry)

    init = (tuple(jnp.full((th, 1), -1e30, F32) for _ in range(nsplit)),
            tuple(jnp.zeros((th, 2 * hd), F32) for _ in range(nsplit)))
    _, accs = lax.fori_loop(0, lax.shift_right_logical(nblk + 1, 1), body, init)
    for h in range(nsplit):
        o = accs[h][:, 0:hd] / accs[h][:, hd:2 * hd]
        gate = _silu(g_ref[h * th:(h + 1) * th, :].astype(F32))
        o_ref[h * th:(h + 1) * th, :] = (o * gate).astype(o_ref.dtype)


def _fox_attention(big, k_aug, q_aug, bsz, seq, tq=512, tk=512, nsplit=2):
    n = big.shape[0]
    nq = seq // tq
    qb, kb, vb, gb = EV_Q // LANE, EV_K // LANE, EV_V // LANE, EV_GA // LANE
    return pl.pallas_call(
        functools.partial(_fox_kernel, tq=tq, tk=tk, nsplit=nsplit),
        out_shape=jax.ShapeDtypeStruct((n, FOX_WIDTH), BF16),
        grid=(bsz, FOX_HEADS, nq),
        in_specs=[pl.BlockSpec((tq, LANE), lambda b, h, i: (b * nq + i, qb + h)),
                  pl.BlockSpec((tq, LANE), lambda b, h, i: (b * nq + i, h)),
                  pl.BlockSpec((seq, LANE), lambda b, h, i: (b, kb + h)),
                  pl.BlockSpec((seq, LANE), lambda b, h, i: (b, h)),
                  pl.BlockSpec((seq, LANE), lambda b, h, i: (b, vb + h)),
                  pl.BlockSpec((tq, LANE), lambda b, h, i: (b * nq + i, gb + h))],
        out_specs=pl.BlockSpec((tq, LANE), lambda b, h, i: (b * nq + i, h)),
        compiler_params=_cparams("parallel", "parallel", "arbitrary"),
        name="fox_attention",
    )(big, q_aug, big, k_aug, big, big)


def _ssd_kernel(x_ref, b_ref, c_ref, z_ref, dtc_ref, dtr_ref,
                wx_ref, wb_ref, wc_ref, bx_ref, bb_ref, bc_ref,
                dbc_ref, dbr_ref, alc_ref, alr_ref, e_ref, e128_ref, dsk_ref, nw_ref,
                o_ref, pad_ref, st_ref, y_ref):
    lc = SSD_CHUNK
    gw = SSD_GW
    cw = gw + 2 * SSD_STATE

    @pl.when(pl.program_id(2) == 0)
    def _():
        pad_ref[0:8, :] = jnp.zeros((8, cw), F32)
        st_ref[...] = jnp.zeros_like(st_ref)

    pad_ref[8:8 + lc, 0:gw] = x_ref[...].astype(F32)
    pad_ref[8:8 + lc, gw:gw + SSD_STATE] = b_ref[...].astype(F32)
    pad_ref[8:8 + lc, gw + SSD_STATE:cw] = c_ref[...].astype(F32)
    w_all = jnp.concatenate([wx_ref[...], wb_ref[...], wc_ref[...]], axis=1)
    b_all = jnp.concatenate([bx_ref[...], bb_ref[...], bc_ref[...]], axis=1)
    conv = b_all + w_all[0:1, :] * pad_ref[5:5 + lc, :]
    for kk in range(1, SSD_CONV):
        conv = conv + w_all[kk:kk + 1, :] * pad_ref[5 + kk:5 + kk + lc, :]
    pad_ref[0:8, :] = pad_ref[lc:lc + 8, :]
    conv = _silu(conv)
    xc = conv[:, 0:gw]
    bm = conv[:, gw:gw + SSD_STATE].astype(BF16)
    cm = conv[:, gw + SSD_STATE:cw].astype(BF16)

    ri = _iota2((lc, lc), 0)
    ci = _iota2((lc, lc), 1)
    causal = ri >= ci
    ltri = causal.astype(BF16)
    utri = (ri <= ci).astype(BF16)

    dt_c = _softplus(dtc_ref[...] + dbc_ref[...])
    a_c = dt_c * (-jnp.exp(alc_ref[...]))
    acum_c = _dot01_l(ltri, a_c)
    dt_r = _softplus(dtr_ref[...] + dbr_ref[...])
    a_r = dt_r * (-jnp.exp(alr_ref[...]))
    acum_r = _dot01_r(a_r, utri)

    e = e_ref[...]
    dt_e = _dot01_r(dt_c, e)
    acum_e = _dot01_r(acum_c, e)
    acol = _dot01_r(acum_c, e128_ref[...])
    alast_e = acum_e[lc - 1:lc, :]

    xd = xc * dt_e
    xd_b = xd.astype(BF16)
    cb = _dot_nt(cm, bm)
    for j in range(SSD_HPG):
        seg = acol[:, j * LANE:(j + 1) * LANE] - acum_r[j:j + 1, :]
        ldec = jnp.exp(jnp.where(causal, seg, -jnp.inf))
        mj = (cb * ldec).astype(BF16)
        y_ref[:, j * SSD_HEAD_DIM:(j + 1) * SSD_HEAD_DIM] = _dot(
            mj, xd_b[:, j * SSD_HEAD_DIM:(j + 1) * SSD_HEAD_DIM])

    st_prev = st_ref[...]
    y_off = _dot(cm, st_prev.astype(BF16)) * jnp.exp(acum_e)
    wgt = (xd * jnp.exp(alast_e - acum_e)).astype(BF16)
    st_ref[...] = st_prev * jnp.exp(alast_e) + _dot_tn(bm, wgt)

    y = y_ref[...] + y_off + dsk_ref[...] * xc
    y = y * _silu(z_ref[...].astype(F32))
    ms = jnp.mean(y * y, axis=-1, keepdims=True)
    o_ref[...] = (y * lax.rsqrt(ms + NORM_EPS) * nw_ref[...]).astype(o_ref.dtype)


def _ssd(big, small, small_t, conv_w, conv_b, dt_bias, a_log, d_skip, norm_w, bsz, seq):
    n = big.shape[0]
    lc = SSD_CHUNK
    nc = seq // lc
    gw = SSD_GW
    f32 = lambda a: a.astype(F32)
    def col_layout(vec):
        out = jnp.zeros((SSD_GROUPS, 1, LANE), F32)
        return out.at[:, 0, :SSD_HPG].set(f32(vec).reshape(SSD_GROUPS, SSD_HPG))
    def row_layout(vec):
        return jnp.broadcast_to(f32(vec).reshape(SSD_GROUPS, SSD_HPG, 1), (SSD_GROUPS, SSD_HPG, lc))
    expand = (jnp.arange(LANE)[:, None] == (jnp.arange(gw) // SSD_HEAD_DIM)[None, :]).astype(BF16)
    expand128 = (jnp.arange(LANE)[:, None] == (jnp.arange(SSD_HPG * LANE) // LANE)[None, :]).astype(BF16)
    xb = EV_X // gw
    bb = EV_B // SSD_STATE
    cb = EV_C // SSD_STATE
    zb = EV_Z // gw
    row = lambda b, g, c: b * nc + c
    cw_x = lambda b, g, c: (0, g)
    cw_b = lambda b, g, c: (0, SSD_WIDTH // SSD_STATE + g)
    cw_c = lambda b, g, c: (0, SSD_WIDTH // SSD_STATE + SSD_GROUPS + g)
    gsel3 = lambda b, g, c: (g, 0, 0)
    return pl.pallas_call(
        _ssd_kernel,
        out_shape=jax.ShapeDtypeStruct((n, SSD_WIDTH), BF16),
        grid=(bsz, SSD_GROUPS, nc),
        in_specs=[
            pl.BlockSpec((lc, gw), lambda b, g, c: (row(b, g, c), xb + g)),
            pl.BlockSpec((lc, SSD_STATE), lambda b, g, c: (row(b, g, c), bb + g)),
            pl.BlockSpec((lc, SSD_STATE), lambda b, g, c: (row(b, g, c), cb + g)),
            pl.BlockSpec((lc, gw), lambda b, g, c: (row(b, g, c), zb + g)),
            pl.BlockSpec((lc, LANE), lambda b, g, c: (row(b, g, c), 1 + g)),
            pl.BlockSpec((8, lc), lambda b, g, c: (1 + g, row(b, g, c))),
            pl.BlockSpec((SSD_CONV, gw), cw_x),
            pl.BlockSpec((SSD_CONV, SSD_STATE), cw_b),
            pl.BlockSpec((SSD_CONV, SSD_STATE), cw_c),
            pl.BlockSpec((1, gw), cw_x),
            pl.BlockSpec((1, SSD_STATE), cw_b),
            pl.BlockSpec((1, SSD_STATE), cw_c),
            pl.BlockSpec((None, 1, LANE), gsel3),
            pl.BlockSpec((None, SSD_HPG, lc), gsel3),
            pl.BlockSpec((None, 1, LANE), gsel3),
            pl.BlockSpec((None, SSD_HPG, lc), gsel3),
            pl.BlockSpec((LANE, gw), lambda b, g, c: (0, 0)),
            pl.BlockSpec((LANE, SSD_HPG * LANE), lambda b, g, c: (0, 0)),
            pl.BlockSpec((1, gw), lambda b, g, c: (0, g)),
            pl.BlockSpec((1, gw), lambda b, g, c: (0, g)),
        ],
        out_specs=pl.BlockSpec((lc, gw), lambda b, g, c: (row(b, g, c), g)),
        scratch_shapes=[pltpu.VMEM((lc + 8, gw + 2 * SSD_STATE), F32),
                        pltpu.VMEM((SSD_STATE, gw), F32),
                        pltpu.VMEM((lc, gw), F32)],
        compiler_params=_cparams("parallel", "parallel", "arbitrary"),
        name="ssd_scan",
    )(big, big, big, big, small, small_t,
      f32(conv_w), f32(conv_w), f32(conv_w),
      f32(conv_b).reshape(1, -1), f32(conv_b).reshape(1, -1), f32(conv_b).reshape(1, -1),
      col_layout(dt_bias), row_layout(dt_bias), col_layout(a_log), row_layout(a_log),
      expand, expand128,
      jnp.repeat(f32(d_skip), SSD_HEAD_DIM).reshape(1, SSD_WIDTH),
      f32(norm_w).reshape(1, SSD_WIDTH))


def _hgrn_kernel(q_ref, i_ref, gt_ref, f_ref, lbl_ref, nw_ref, o_ref, st_ref, row_ref, *, tb, layer):
    lc = HGRN_CHUNK
    sub = HGRN_SUB
    nsub = lc // sub
    dk = HGRN_DK

    @pl.when(pl.program_id(2) == 0)
    def _():
        st_ref[...] = jnp.zeros_like(st_ref)

    lbl = lbl_ref[...].astype(F32)
    ex = jnp.exp(lbl - jnp.max(lbl, axis=0, keepdims=True))
    pr = ex / jnp.sum(ex, axis=0, keepdims=True)
    lb = jnp.zeros((1, dk), F32)
    for r in range(1, layer + 1):
        lb = lb + pr[r:r + 1, :]
    log_lb = jnp.log(lb)
    log_1mlb = jnp.log(1.0 - lb)

    ri = _iota2((lc, lc), 0)
    ci = _iota2((lc, lc), 1)
    sub_shift = sub.bit_length() - 1
    sub_start = jnp.left_shift(jnp.right_shift(ri, sub_shift), sub_shift)
    l_within = ((ci >= sub_start) & (ci <= ri)).astype(BF16)
    l_before = (ci < sub_start).astype(BF16)
    l_cum = jnp.concatenate([l_within, l_before], axis=0)
    ncat = sub * (nsub * (nsub - 1) // 2)
    seg_of_col = jnp.zeros((lc, ncat), jnp.int32)
    off = 0
    for i in range(1, nsub):
        cols = _iota2((lc, ncat), 1)
        seg_of_col = jnp.where((cols >= off) & (cols < off + sub * i), i, seg_of_col)
        off += sub * i
    cat_mask = jnp.right_shift(_iota2((lc, ncat), 0), sub_shift) == seg_of_col
    ones_b = jnp.ones((dk, dk), BF16)
    tsub = _iota2((sub, dk), 0)

    nchunk = tb // lc
    rows = lambda a, c: a[c * lc:(c + 1) * lc]

    qf = _silu(q_ref[...].astype(F32))
    ff = f_ref[...]
    v = i_ref[...].astype(F32)
    v_b = i_ref[...]
    ea = jnp.exp(-jnp.abs(ff))
    log_sig = jnp.minimum(ff, 0.0) - jnp.log(1.0 + ea)
    sig_neg = jnp.where(ff >= 0.0, ea, 1.0) / (1.0 + ea)
    t2 = log_1mlb + log_sig
    mx = jnp.maximum(log_lb, t2)
    g = mx + jnp.log(1.0 + jnp.exp(jnp.minimum(log_lb, t2) - mx))
    kin = (1.0 - lb) * sig_neg

    g2 = g * LOG2E
    cums = _dot01_l(l_cum, jnp.concatenate([rows(g2, c) for c in range(nchunk)], axis=1))
    bw = jnp.concatenate([cums[0:lc, c * dk:(c + 1) * dk] for c in range(nchunk)], axis=0)
    beta = jnp.concatenate([cums[lc:2 * lc, c * dk:(c + 1) * dk] for c in range(nchunk)], axis=0)
    bfull = bw + beta
    qd = qf * jnp.exp2(bw)
    qd_b = qd.astype(BF16)
    qs_b = (qd * jnp.exp2(beta)).astype(BF16)
    blast = [bfull[(c + 1) * lc - 1:(c + 1) * lc, :] for c in range(nchunk)]
    blast_rows = jnp.concatenate([jnp.broadcast_to(b, (lc, dk)) for b in blast], axis=0)
    kd_b = (kin * jnp.exp2(blast_rows - bfull)).astype(BF16)

    upd = [_dot_tn(rows(v_b, c), rows(kd_b, c)) for c in range(nchunk)]
    kcats = []
    vcats = []
    for c in range(nchunk):
        kin_c, beta_c, bf_c, vb_c = rows(kin, c), rows(beta, c), rows(bfull, c), rows(v_b, c)
        kcat = []
        vcat = []
        for i in range(1, nsub):
            beta_i = beta_c[i * sub:i * sub + 1, :]
            kcat.append(kin_c[0:i * sub, :] * jnp.exp2(beta_i - bf_c[0:i * sub, :]))
            vcat.append(vb_c[0:i * sub, :])
        kcats.append(jnp.concatenate(kcat, axis=0).astype(BF16))
        vcats.append(jnp.concatenate(vcat, axis=0))
    a_raw = [_dot_nt(rows(qd_b, c), kcats[c]) for c in range(nchunk)]
    o_cat = [_dot(jnp.where(cat_mask, a_raw[c], 0.0).astype(BF16), vcats[c]) for c in range(nchunk)]

    nblk = tb // sub
    row_ref[0] = bw
    row_ref[1] = kin
    row_ref[2] = v
    prods = []
    for blk in range(nblk):
        sl = slice(blk * sub, (blk + 1) * sub)
        q_s, b_s = qf[sl], bw[sl]
        for s in range(sub):
            r = blk * sub + s
            diff = b_s - row_ref[0, r:r + 1, :]
            if s > 0:
                diff = jnp.where(tsub >= s, diff, -jnp.inf)
            prods.append(q_s * row_ref[1, r:r + 1, :] * jnp.exp2(diff))
    rs = _dot(jnp.concatenate(prods, axis=0).astype(BF16), ones_b)
    o_diag = []
    for blk in range(nblk):
        base = blk * sub * sub
        r = blk * sub
        od = rs[base:base + sub, :] * row_ref[2, r:r + 1, :]
        for s in range(1, sub):
            od = od + rs[base + s * sub:base + (s + 1) * sub, :] * row_ref[2, r + s:r + s + 1, :]
        o_diag.append(od)

    st = st_ref[...]
    o_state = []
    for c in range(nchunk):
        o_state.append(_dot_nt(rows(qs_b, c), st.astype(BF16)))
        st = st * jnp.exp2(blast[c]) + upd[c]
    st_ref[...] = st

    o = jnp.concatenate(o_state, axis=0) + jnp.concatenate(o_cat, axis=0) + jnp.concatenate(o_diag, axis=0)
    ms = jnp.mean(o * o, axis=-1, keepdims=True)
    o = o * lax.rsqrt(ms + NORM_EPS) * nw_ref[...]
    o = o * _silu(gt_ref[...].astype(F32))
    o_ref[...] = o.astype(o_ref.dtype)


def _hgrn(big, fgate, lb_logits, norm_w, layer, bsz, seq, tb=1024):
    n = big.shape[0]
    nt = seq // tb
    nl = lb_logits.shape[0]
    qb, ib, gb = OD_Q // LANE, OD_I // LANE, OD_G // LANE
    row = lambda b, h, t: b * nt + t
    return pl.pallas_call(
        functools.partial(_hgrn_kernel, tb=tb, layer=layer),
        out_shape=jax.ShapeDtypeStruct((n, HGRN_WIDTH), BF16),
        grid=(bsz, HGRN_HEADS, nt),
        in_specs=[pl.BlockSpec((tb, LANE), lambda b, h, t: (row(b, h, t), qb + h)),
                  pl.BlockSpec((tb, LANE), lambda b, h, t: (row(b, h, t), ib + h)),
                  pl.BlockSpec((tb, LANE), lambda b, h, t: (row(b, h, t), gb + h)),
                  pl.BlockSpec((tb, LANE), lambda b, h, t: (row(b, h, t), h)),
                  pl.BlockSpec((nl, LANE), lambda b, h, t: (0, h)),
                  pl.BlockSpec((1, LANE), lambda b, h, t: (0, 0))],
        out_specs=pl.BlockSpec((tb, LANE), lambda b, h, t: (row(b, h, t), h)),
        scratch_shapes=[pltpu.VMEM((HGRN_DK, HGRN_DK), F32), pltpu.VMEM((3, tb, LANE), F32)],
        compiler_params=_cparams("parallel", "parallel", "arbitrary"),
        name="hgrn2_scan",
    )(big, big, big, fgate, lb_logits.astype(F32), norm_w.astype(F32).reshape(1, LANE))


def _even_weights(w_in):
    w = w_in.astype(F32)
    q, k, v, ga = (w[:, i * 1024:(i + 1) * 1024] for i in range(4))
    q = q * (FOX_HEAD_DIM ** -0.5 * LOG2E)
    fl = w[:, 4096:4104]
    z = w[:, 4104:5128]
    xbc = w[:, 5128:6664]
    dt = w[:, 6664:6680]
    main = jnp.concatenate([q, k, v, ga, z, xbc], axis=1).astype(BF16)
    small = jnp.zeros((D_MODEL, 3 * LANE), F32).at[:, 0:8].set(fl)
    small = small.at[:, LANE:LANE + 8].set(dt[:, 0:8]).at[:, 2 * LANE:2 * LANE + 8].set(dt[:, 8:16])
    small_t = jnp.zeros((32, D_MODEL), F32).at[0:8].set(fl.T).at[8:24].set(dt.T)
    return main, small.astype(BF16), small_t.astype(BF16)


def _even_layer(h, u, w_in, w_out, f_bias, conv_w, conv_b, dt_bias, a_log, d_skip, ssd_nw,
                next_norm_w, final, bsz, seq):
    w_main, w_small, w_small_t = _even_weights(w_in)
    big = _matmul(u, w_main, BF16, tm=1024, tn=512, name="even_in_proj")
    small, small_t = _small_proj(u, w_small, w_small_t)
    k_aug, q_aug = _fox_gate_cumsum(small, f_bias, bsz, seq)
    o_a = _fox_attention(big, k_aug, q_aug, bsz, seq)
    y = _ssd(big, small, small_t, conv_w, conv_b, dt_bias, a_log, d_skip, ssd_nw, bsz, seq)
    return _out_proj([o_a, y], w_out.astype(BF16), h, next_norm_w, final)


def _odd_layer(h, u, w_in, w_out, lb_logits, hgrn_nw, layer, next_norm_w, final, bsz, seq):
    w = w_in.astype(F32)
    w_main = jnp.concatenate([w[:, 0:2048], w[:, 4096:6144], w[:, 6144:8192]], axis=1).astype(BF16)
    w_f = w[:, 2048:4096].astype(BF16)
    big = _matmul(u, w_main, BF16, tm=1024, tn=512, name="odd_in_proj")
    fgate = _matmul(u, w_f, F32, tm=1024, tn=512, name="odd_gate_proj")
    o = _hgrn(big, fgate, lb_logits, hgrn_nw, layer, bsz, seq)
    return _out_proj([o], w_out.astype(BF16), h, next_norm_w, final)


def kernel(x, norm_w, final_norm_w, even_w_in, even_w_out, fox_f_bias, ssd_conv_w, ssd_conv_b, ssd_dt_bias,
           ssd_A_log, ssd_D, ssd_norm_w, odd_w_in, odd_w_out, hgrn_lb_logits, hgrn_norm_w):
    bsz, seq, d = x.shape
    depth = norm_w.shape[0]
    assert d == D_MODEL and seq % 1024 == 0, (x.shape,)
    h = x.reshape(bsz * seq, d).astype(F32)
    u = _rmsnorm(h, norm_w[0].astype(F32))
    out = None
    for layer in range(depth):
        final = layer == depth - 1
        next_w = (final_norm_w if final else norm_w[layer + 1]).astype(F32)
        if layer % 2 == 0:
            e = layer // 2
            res = _even_layer(h, u, even_w_in[e], even_w_out[e], fox_f_bias[e], ssd_conv_w[e], ssd_conv_b[e],
                              ssd_dt_bias[e], ssd_A_log[e], ssd_D[e], ssd_norm_w[e], next_w, final, bsz, seq)
        else:
            o = layer // 2
            res = _odd_layer(h, u, odd_w_in[o], odd_w_out[o], hgrn_lb_logits, hgrn_norm_w[o], o,
                             next_w, final, bsz, seq)
        if final:
            out = res
        else:
            h, u = res
    return out.reshape(bsz, seq, d).astype(x.dtype)
```

```python
import functools

import jax
import jax.numpy as jnp
from jax import lax
from jax.experimental import pallas as pl
from jax.experimental.pallas import tpu as pltpu

F32 = jnp.float32
BF16 = jnp.bfloat16

D_MODEL = 1024
D_MIX = 2048
NORM_EPS = 1e-5

FOX_WIDTH = 1024
FOX_HEAD_DIM = 128
FOX_HEADS = 8

SSD_WIDTH = 1024
SSD_HEAD_DIM = 64
SSD_HEADS = 16
SSD_GROUPS = 2
SSD_HPG = 8
SSD_STATE = 128
SSD_CONV = 4
SSD_CHUNK = 128
SSD_GW = SSD_WIDTH // SSD_GROUPS

HGRN_WIDTH = 2048
HGRN_DK = 128
HGRN_HEADS = 16
HGRN_CHUNK = 64
HGRN_SUB = 8

LOG2E = 1.4426950408889634
LANE = 128
VMEM_LIMIT = 48 * 1024 * 1024

EV_Q, EV_K, EV_V, EV_GA, EV_Z, EV_X = 0, 1024, 2048, 3072, 4096, 5120
EV_B = EV_X + SSD_WIDTH
EV_C = EV_B + SSD_GROUPS * SSD_STATE
EV_COLS = EV_C + SSD_GROUPS * SSD_STATE
OD_Q, OD_I, OD_G = 0, 2048, 4096
OD_COLS = 6144


def _cparams(*sem):
    return pltpu.CompilerParams(dimension_semantics=sem, vmem_limit_bytes=VMEM_LIMIT)


def _dot(a, b):
    return jnp.dot(a, b, preferred_element_type=F32)


def _dot_nt(a, b):
    return lax.dot_general(a, b, (((1,), (1,)), ((), ())), preferred_element_type=F32)


def _dot_tn(a, b):
    return lax.dot_general(a, b, (((0,), (0,)), ((), ())), preferred_element_type=F32)


def _split3(x):
    hi = x.astype(BF16)
    r1 = x - hi.astype(F32)
    mid = r1.astype(BF16)
    lo = (r1 - mid.astype(F32)).astype(BF16)
    return hi, mid, lo


def _dot01_l(m01, x):
    hi, mid, lo = _split3(x)
    return _dot(m01, hi) + _dot(m01, mid) + _dot(m01, lo)


def _dot01_r(x, m01):
    hi, mid, lo = _split3(x)
    return _dot(hi, m01) + _dot(mid, m01) + _dot(lo, m01)


def _log_sigmoid(x):
    return jnp.minimum(x, 0.0) - jnp.log(1.0 + jnp.exp(-jnp.abs(x)))


def _softplus(x):
    return jnp.maximum(x, 0.0) + jnp.log(1.0 + jnp.exp(-jnp.abs(x)))


def _sigmoid(x):
    return 1.0 / (1.0 + jnp.exp(-x))


def _silu(x):
    return x * _sigmoid(x)


def _iota2(shape, dim):
    return lax.broadcasted_iota(jnp.int32, shape, dim)


def _rms_kernel(x_ref, w_ref, o_ref):
    x = x_ref[...]
    ms = jnp.mean(x * x, axis=-1, keepdims=True)
    o_ref[...] = (x * lax.rsqrt(ms + NORM_EPS) * w_ref[...]).astype(o_ref.dtype)


def _rmsnorm(x, w, tm=512):
    n, d = x.shape
    return pl.pallas_call(
        _rms_kernel,
        out_shape=jax.ShapeDtypeStruct((n, d), BF16),
        grid=(n // tm,),
        in_specs=[pl.BlockSpec((tm, d), lambda i: (i, 0)),
                  pl.BlockSpec((1, d), lambda i: (0, 0))],
        out_specs=pl.BlockSpec((tm, d), lambda i: (i, 0)),
        compiler_params=_cparams("parallel"),
        name="rmsnorm",
    )(x, w.reshape(1, d))


def _mm_kernel(a_ref, w_ref, o_ref):
    o_ref[...] = _dot(a_ref[...], w_ref[...]).astype(o_ref.dtype)


def _matmul(a, w, out_dtype, tm, tn, name):
    n, k = a.shape
    c = w.shape[1]
    return pl.pallas_call(
        _mm_kernel,
        out_shape=jax.ShapeDtypeStruct((n, c), out_dtype),
        grid=(n // tm, c // tn),
        in_specs=[pl.BlockSpec((tm, k), lambda i, j: (i, 0)),
                  pl.BlockSpec((k, tn), lambda i, j: (0, j))],
        out_specs=pl.BlockSpec((tm, tn), lambda i, j: (i, j)),
        compiler_params=_cparams("parallel", "arbitrary"),
        name=name,
    )(a, w)


def _small_proj_kernel(a_ref, w_ref, wt_ref, o_ref, ot_ref):
    a = a_ref[...]
    o_ref[...] = _dot(a, w_ref[...])
    ot_ref[...] = _dot_nt(wt_ref[...], a)


def _small_proj(a, w, wt, tm=512):
    n, k = a.shape
    c = w.shape[1]
    return pl.pallas_call(
        _small_proj_kernel,
        out_shape=(jax.ShapeDtypeStruct((n, c), F32), jax.ShapeDtypeStruct((32, n), F32)),
        grid=(n // tm,),
        in_specs=[pl.BlockSpec((tm, k), lambda i: (i, 0)),
                  pl.BlockSpec((k, c), lambda i: (0, 0)),
                  pl.BlockSpec((32, k), lambda i: (0, 0))],
        out_specs=(pl.BlockSpec((tm, c), lambda i: (i, 0)),
                   pl.BlockSpec((32, tm), lambda i: (0, i))),
        compiler_params=_cparams("parallel"),
        name="gate_proj",
    )(a, w, wt)


def _out_kernel(*refs, n_in, final):
    a_refs = refs[:n_in]
    w_ref, h_ref, nw_ref = refs[n_in:n_in + 3]
    outs = refs[n_in + 3:]
    acc = h_ref[...]
    k0 = 0
    for a_ref in a_refs:
        kk = a_ref.shape[1]
        acc = acc + _dot(a_ref[...], w_ref[k0:k0 + kk, :])
        k0 += kk
    ms = jnp.mean(acc * acc, axis=-1, keepdims=True)
    normed = acc * lax.rsqrt(ms + NORM_EPS) * nw_ref[...]
    if final:
        outs[0][...] = normed
    else:
        outs[0][...] = acc
        outs[1][...] = normed.astype(BF16)


def _out_proj(acts, w, h, next_norm_w, final, tm=512):
    n, d = h.shape
    kt = w.shape[0]
    in_specs = [pl.BlockSpec((tm, a.shape[1]), lambda i: (i, 0)) for a in acts]
    in_specs += [pl.BlockSpec((kt, d), lambda i: (0, 0)),
                 pl.BlockSpec((tm, d), lambda i: (i, 0)),
                 pl.BlockSpec((1, d), lambda i: (0, 0))]
    row = pl.BlockSpec((tm, d), lambda i: (i, 0))
    if final:
        out_shape = jax.ShapeDtypeStruct((n, d), F32)
        out_specs = row
    else:
        out_shape = (jax.ShapeDtypeStruct((n, d), F32), jax.ShapeDtypeStruct((n, d), BF16))
        out_specs = (row, row)
    return pl.pallas_call(
        functools.partial(_out_kernel, n_in=len(acts), final=final),
        out_shape=out_shape,
        grid=(n // tm,),
        in_specs=in_specs,
        out_specs=out_specs,
        compiler_params=_cparams("parallel"),
        name="out_proj",
    )(*acts, w, h, next_norm_w.reshape(1, d))


FOX_AUG = 3


def _foxgate_kernel(s_ref, bc_ref, pk_ref, pq_ref, ka_ref, qa_ref, carry_c, *, tc):
    @pl.when(pl.program_id(1) == 0)
    def _():
        carry_c[...] = jnp.zeros_like(carry_c)

    ltri = (_iota2((tc, tc), 1) <= _iota2((tc, tc), 0)).astype(BF16)
    lc = _log_sigmoid(s_ref[...] + bc_ref[...])
    cc = _dot01_l(ltri, lc) + carry_c[0:1, :]
    carry_c[...] = jnp.broadcast_to(cc[tc - 1:tc, :], carry_c.shape)

    pieces = jnp.concatenate(_split3(cc * LOG2E), axis=1)
    lane = jnp.bitwise_and(_iota2((tc, FOX_HEADS * LANE), 1), LANE - 1)
    ones_k = jnp.where((lane >= FOX_AUG) & (lane < 2 * FOX_AUG), 1.0, 0.0)
    ones_q = jnp.where(lane < FOX_AUG, 1.0, 0.0)
    ka_ref[...] = (ones_k - _dot(pieces, pk_ref[...])).astype(BF16)
    qa_ref[...] = (ones_q + _dot(pieces, pq_ref[...])).astype(BF16)


def _fox_gate_cumsum(small, f_bias, bsz, seq, tc=512):
    n = small.shape[0]
    nt = seq // tc
    bias_c = jnp.zeros((1, LANE), F32).at[0, :FOX_HEADS].set(f_bias.astype(F32))
    src = jnp.arange(FOX_AUG * LANE)
    dst = jnp.arange(FOX_HEADS * LANE)
    same_head = (src[:, None] % LANE) == (dst[None, :] // LANE)
    piece = src[:, None] // LANE
    place_k = (same_head & (dst[None, :] % LANE == piece)).astype(BF16)
    place_q = (same_head & (dst[None, :] % LANE == piece + FOX_AUG)).astype(BF16)
    wide = FOX_HEADS * LANE
    return pl.pallas_call(
        functools.partial(_foxgate_kernel, tc=tc),
        out_shape=(jax.ShapeDtypeStruct((n, wide), BF16), jax.ShapeDtypeStruct((n, wide), BF16)),
        grid=(bsz, nt),
        in_specs=[pl.BlockSpec((tc, LANE), lambda b, j: (b * nt + j, 0)),
                  pl.BlockSpec((1, LANE), lambda b, j: (0, 0)),
                  pl.BlockSpec((FOX_AUG * LANE, wide), lambda b, j: (0, 0)),
                  pl.BlockSpec((FOX_AUG * LANE, wide), lambda b, j: (0, 0))],
        out_specs=(pl.BlockSpec((tc, wide), lambda b, j: (b * nt + j, 0)),
                   pl.BlockSpec((tc, wide), lambda b, j: (b * nt + j, 0))),
        scratch_shapes=[pltpu.VMEM((8, LANE), F32)],
        compiler_params=_cparams("parallel", "arbitrary"),
        name="fox_gate_cumsum",
    )(small, bias_c, place_k, place_q)


def _fox_kernel(q_ref, qa_ref, k_ref, ka_ref, v_ref, g_ref, o_ref, *, tq, tk, nsplit):
    qi = pl.program_id(2)
    hd = FOX_HEAD_DIM
    th = tq // nsplit
    qq = [jnp.concatenate([q_ref[h * th:(h + 1) * th, :], qa_ref[h * th:(h + 1) * th, :]], axis=1)
          for h in range(nsplit)]
    ones_v = jnp.ones((tk, hd), BF16)
    nblk = (qi + 1) * (tq // tk)
    row0 = qi * tq + _iota2((th, tk), 0)
    col0 = _iota2((th, tk), 1)

    def logits(kj):
        start = pl.multiple_of(kj * tk, tk)
        kk = jnp.concatenate([k_ref[pl.ds(start, tk), :], ka_ref[pl.ds(start, tk), :]], axis=1)
        col = col0 + kj * tk
        return tuple(jnp.where(row0 + h * th >= col, _dot_nt(qq[h], kk), -jnp.inf) for h in range(nsplit))

    def update(kj, ss, ms, accs):
        start = pl.multiple_of(kj * tk, tk)
        vv = jnp.concatenate([v_ref[pl.ds(start, tk), :], ones_v], axis=1)
        ms_new, accs_new = [], []
        for s, m, acc in zip(ss, ms, accs):
            m_new = jnp.maximum(m, jnp.max(s, axis=-1, keepdims=True))
            p = jnp.exp2(s - m_new)
            accs_new.append(jnp.exp2(m - m_new) * acc + _dot(p.astype(BF16), vv))
            ms_new.append(m_new)
        return tuple(ms_new), tuple(accs_new)

    def body(i, carry):
        ss0 = logits(2 * i)
        ss1 = logits(2 * i + 1)
        carry = update(2 * i, ss0, *carry)
        return update(2 * i + 1, ss1, *carry)

    init = (tuple(jnp.full((th, 1), -1e30, F32) for _ in range(nsplit)),
            tuple(jnp.zeros((th, 2 * hd), F32) for _ in range(nsplit)))
    _, accs = lax.fori_loop(0, lax.shift_right_logical(nblk + 1, 1), body, init)
    for h in range(nsplit):
        o = accs[h][:, 0:hd] / accs[h][:, hd:2 * hd]
        gate = _silu(g_ref[h * th:(h + 1) * th, :].astype(F32))
        o_ref[h * th:(h + 1) * th, :] = (o * gate).astype(o_ref.dtype)


def _fox_attention(big, k_aug, q_aug, bsz, seq, tq=512, tk=512, nsplit=1):
    n = big.shape[0]
    nq = seq // tq
    qb, kb, vb, gb = EV_Q // LANE, EV_K // LANE, EV_V // LANE, EV_GA // LANE
    return pl.pallas_call(
        functools.partial(_fox_kernel, tq=tq, tk=tk, nsplit=nsplit),
        out_shape=jax.ShapeDtypeStruct((n, FOX_WIDTH), BF16),
        grid=(bsz, FOX_HEADS, nq),
        in_specs=[pl.BlockSpec((tq, LANE), lambda b, h, i: (b * nq + i, qb + h)),
                  pl.BlockSpec((tq, LANE), lambda b, h, i: (b * nq + i, h)),
                  pl.BlockSpec((seq, LANE), lambda b, h, i: (b, kb + h)),
                  pl.BlockSpec((seq, LANE), lambda b, h, i: (b, h)),
                  pl.BlockSpec((seq, LANE), lambda b, h, i: (b, vb + h)),
                  pl.BlockSpec((tq, LANE), lambda b, h, i: (b * nq + i, gb + h))],
        out_specs=pl.BlockSpec((tq, LANE), lambda b, h, i: (b * nq + i, h)),
        compiler_params=_cparams("parallel", "parallel", "arbitrary"),
        name="fox_attention",
    )(big, q_aug, big, k_aug, big, big)


def _ssd_kernel(x_ref, b_ref, c_ref, z_ref, dtc_ref, dtr_ref,
                wx_ref, wb_ref, wc_ref, bx_ref, bb_ref, bc_ref,
                dbc_ref, dbr_ref, alc_ref, alr_ref, e_ref, e128_ref, dsk_ref, nw_ref,
                o_ref, pad_ref, st_ref, y_ref, *, nb):
    lc = SSD_CHUNK
    gw = SSD_GW
    cw = gw + 2 * SSD_STATE
    rows = nb * lc
    blk = lambda a, i: a[i * lc:(i + 1) * lc]
    chunks = range(nb)

    @pl.when(pl.program_id(2) == 0)
    def _():
        pad_ref[0:lc, :] = jnp.zeros((lc, cw), BF16)
        st_ref[...] = jnp.zeros_like(st_ref)

    pad_ref[lc:lc + rows, 0:gw] = x_ref[...]
    pad_ref[lc:lc + rows, gw:gw + SSD_STATE] = b_ref[...]
    pad_ref[lc:lc + rows, gw + SSD_STATE:cw] = c_ref[...]
    both = pad_ref[...]
    pad_ref[0:lc, :] = both[rows:rows + lc]
    w_all = jnp.concatenate([wx_ref[...], wb_ref[...], wc_ref[...]], axis=1)
    b_all = jnp.concatenate([bx_ref[...], bb_ref[...], bc_ref[...]], axis=1)
    src = _iota2((lc, 2 * lc), 1) - _iota2((lc, 2 * lc), 0)
    shifted = []
    for kk in range(SSD_CONV - 1):
        shift = (src == lc - (SSD_CONV - 1) + kk).astype(BF16)
        shifted.append([_dot(shift, both[i * lc:(i + 2) * lc]) for i in chunks])
    xc, bm, cm = [], [], []
    for i in chunks:
        conv = b_all + w_all[SSD_CONV - 1:SSD_CONV, :] * both[(i + 1) * lc:(i + 2) * lc].astype(F32)
        for kk in range(SSD_CONV - 1):
            conv = conv + w_all[kk:kk + 1, :] * shifted[kk][i]
        conv = _silu(conv)
        xc.append(conv[:, 0:gw])
        bm.append(conv[:, gw:gw + SSD_STATE].astype(BF16))
        cm.append(conv[:, gw + SSD_STATE:cw].astype(BF16))

    ri = _iota2((lc, lc), 0)
    ci = _iota2((lc, lc), 1)
    causal = ri >= ci
    ltri = causal.astype(BF16)
    utri = (ri <= ci).astype(BF16)

    dt_c = _softplus(dtc_ref[...] + dbc_ref[...])
    a_c = dt_c * (-jnp.exp(alc_ref[...]))
    dt_r = _softplus(dtr_ref[...] + dbr_ref[...])
    a_r = dt_r * (-jnp.exp(alr_ref[...]))
    acum_c = [_dot01_l(ltri, blk(a_c, i)) for i in chunks]
    acum_r = [_dot01_r(a_r[:, i * lc:(i + 1) * lc], utri) for i in chunks]
    e = e_ref[...]
    dt_e = [_dot01_r(blk(dt_c, i), e) for i in chunks]
    acum_e = [_dot01_r(acum_c[i], e) for i in chunks]
    acol = [_dot01_r(acum_c[i], e128_ref[...]) for i in chunks]
    alast_e = [a[lc - 1:lc, :] for a in acum_e]

    xd = [xc[i] * dt_e[i] for i in chunks]
    xd_b = [x.astype(BF16) for x in xd]
    cb = [_dot_nt(cm[i], bm[i]) for i in chunks]
    wgt = [(xd[i] * jnp.exp(alast_e[i] - acum_e[i])).astype(BF16) for i in chunks]
    upd = [_dot_tn(bm[i], wgt[i]) for i in chunks]
    for i in chunks:
        for j in range(SSD_HPG):
            seg = acol[i][:, j * LANE:(j + 1) * LANE] - acum_r[i][j:j + 1, :]
            ldec = jnp.exp(jnp.where(causal, seg, -jnp.inf))
            mj = (cb[i] * ldec).astype(BF16)
            y_ref[i * lc:(i + 1) * lc, j * SSD_HEAD_DIM:(j + 1) * SSD_HEAD_DIM] = _dot(
                mj, xd_b[i][:, j * SSD_HEAD_DIM:(j + 1) * SSD_HEAD_DIM])

    st = st_ref[...]
    y_off = []
    for i in chunks:
        y_off.append(_dot(cm[i], st.astype(BF16)) * jnp.exp(acum_e[i]))
        st = st * jnp.exp(alast_e[i]) + upd[i]
    st_ref[...] = st

    for i in chunks:
        y = y_ref[i * lc:(i + 1) * lc, :] + y_off[i] + dsk_ref[...] * xc[i]
        y = y * _silu(z_ref[i * lc:(i + 1) * lc, :].astype(F32))
        ms = jnp.mean(y * y, axis=-1, keepdims=True)
        o_ref[i * lc:(i + 1) * lc, :] = (y * lax.rsqrt(ms + NORM_EPS) * nw_ref[...]).astype(o_ref.dtype)


def _ssd(big, small, small_t, conv_w, conv_b, dt_bias, a_log, d_skip, norm_w, bsz, seq, nb=4):
    n = big.shape[0]
    lc = SSD_CHUNK
    rows = nb * lc
    nc = seq // rows
    gw = SSD_GW
    f32 = lambda a: a.astype(F32)
    def col_layout(vec):
        out = jnp.zeros((SSD_GROUPS, 1, LANE), F32)
        return out.at[:, 0, :SSD_HPG].set(f32(vec).reshape(SSD_GROUPS, SSD_HPG))
    def row_layout(vec):
        return jnp.broadcast_to(f32(vec).reshape(SSD_GROUPS, SSD_HPG, 1), (SSD_GROUPS, SSD_HPG, rows))
    expand = (jnp.arange(LANE)[:, None] == (jnp.arange(gw) // SSD_HEAD_DIM)[None, :]).astype(BF16)
    expand128 = (jnp.arange(LANE)[:, None] == (jnp.arange(SSD_HPG * LANE) // LANE)[None, :]).astype(BF16)
    xb = EV_X // gw
    bb = EV_B // SSD_STATE
    cb = EV_C // SSD_STATE
    zb = EV_Z // gw
    row = lambda b, g, c: b * nc + c
    cw_x = lambda b, g, c: (0, g)
    cw_b = lambda b, g, c: (0, SSD_WIDTH // SSD_STATE + g)
    cw_c = lambda b, g, c: (0, SSD_WIDTH // SSD_STATE + SSD_GROUPS + g)
    gsel3 = lambda b, g, c: (g, 0, 0)
    return pl.pallas_call(
        functools.partial(_ssd_kernel, nb=nb),
        out_shape=jax.ShapeDtypeStruct((n, SSD_WIDTH), BF16),
        grid=(bsz, SSD_GROUPS, nc),
        in_specs=[
            pl.BlockSpec((rows, gw), lambda b, g, c: (row(b, g, c), xb + g)),
            pl.BlockSpec((rows, SSD_STATE), lambda b, g, c: (row(b, g, c), bb + g)),
            pl.BlockSpec((rows, SSD_STATE), lambda b, g, c: (row(b, g, c), cb + g)),
            pl.BlockSpec((rows, gw), lambda b, g, c: (row(b, g, c), zb + g)),
            pl.BlockSpec((rows, LANE), lambda b, g, c: (row(b, g, c), 1 + g)),
            pl.BlockSpec((8, rows), lambda b, g, c: (1 + g, row(b, g, c))),
            pl.BlockSpec((SSD_CONV, gw), cw_x),
            pl.BlockSpec((SSD_CONV, SSD_STATE), cw_b),
            pl.BlockSpec((SSD_CONV, SSD_STATE), cw_c),
            pl.BlockSpec((1, gw), cw_x),
            pl.BlockSpec((1, SSD_STATE), cw_b),
            pl.BlockSpec((1, SSD_STATE), cw_c),
            pl.BlockSpec((None, 1, LANE), gsel3),
            pl.BlockSpec((None, SSD_HPG, rows), gsel3),
            pl.BlockSpec((None, 1, LANE), gsel3),
            pl.BlockSpec((None, SSD_HPG, rows), gsel3),
            pl.BlockSpec((LANE, gw), lambda b, g, c: (0, 0)),
            pl.BlockSpec((LANE, SSD_HPG * LANE), lambda b, g, c: (0, 0)),
            pl.BlockSpec((1, gw), lambda b, g, c: (0, g)),
            pl.BlockSpec((1, gw), lambda b, g, c: (0, g)),
        ],
        out_specs=pl.BlockSpec((rows, gw), lambda b, g, c: (row(b, g, c), g)),
        scratch_shapes=[pltpu.VMEM((lc + rows, gw + 2 * SSD_STATE), BF16),
                        pltpu.VMEM((SSD_STATE, gw), F32),
                        pltpu.VMEM((rows, gw), F32)],
        compiler_params=_cparams("parallel", "parallel", "arbitrary"),
        name="ssd_scan",
    )(big, big, big, big, small, small_t,
      f32(conv_w), f32(conv_w), f32(conv_w),
      f32(conv_b).reshape(1, -1), f32(conv_b).reshape(1, -1), f32(conv_b).reshape(1, -1),
      col_layout(dt_bias), row_layout(dt_bias), col_layout(a_log), row_layout(a_log),
      expand, expand128,
      jnp.repeat(f32(d_skip), SSD_HEAD_DIM).reshape(1, SSD_WIDTH),
      f32(norm_w).reshape(1, SSD_WIDTH))


def _hgrn_kernel(q_ref, i_ref, gt_ref, f_ref, lbl_ref, nw_ref, o_ref, st_ref, row_ref, *, tb, layer):
    lc = HGRN_CHUNK
    sub = HGRN_SUB
    nsub = lc // sub
    dk = HGRN_DK

    @pl.when(pl.program_id(2) == 0)
    def _():
        st_ref[...] = jnp.zeros_like(st_ref)

    lbl = lbl_ref[...].astype(F32)
    ex = jnp.exp(lbl - jnp.max(lbl, axis=0, keepdims=True))
    pr = ex / jnp.sum(ex, axis=0, keepdims=True)
    lb = jnp.zeros((1, dk), F32)
    for r in range(1, layer + 1):
        lb = lb + pr[r:r + 1, :]
    log_lb = jnp.log(lb)
    log_1mlb = jnp.log(1.0 - lb)

    ri = _iota2((lc, lc), 0)
    ci = _iota2((lc, lc), 1)
    sub_shift = sub.bit_length() - 1
    sub_start = jnp.left_shift(jnp.right_shift(ri, sub_shift), sub_shift)
    l_within = ((ci >= sub_start) & (ci <= ri)).astype(BF16)
    l_before = (ci < sub_start).astype(BF16)
    l_cum = jnp.concatenate([l_within, l_before], axis=0)
    ncat = sub * (nsub * (nsub - 1) // 2)
    seg_of_col = jnp.zeros((lc, ncat), jnp.int32)
    off = 0
    for i in range(1, nsub):
        cols = _iota2((lc, ncat), 1)
        seg_of_col = jnp.where((cols >= off) & (cols < off + sub * i), i, seg_of_col)
        off += sub * i
    cat_mask = jnp.right_shift(_iota2((lc, ncat), 0), sub_shift) == seg_of_col
    ones_b = jnp.ones((dk, dk), BF16)
    tsub = _iota2((sub, dk), 0)

    nchunk = tb // lc
    rows = lambda a, c: a[c * lc:(c + 1) * lc]

    qf = _silu(q_ref[...].astype(F32))
    ff = f_ref[...]
    v = i_ref[...].astype(F32)
    v_b = i_ref[...]
    ea = jnp.exp(-jnp.abs(ff))
    log_sig = jnp.minimum(ff, 0.0) - jnp.log(1.0 + ea)
    sig_neg = jnp.where(ff >= 0.0, ea, 1.0) / (1.0 + ea)
    t2 = log_1mlb + log_sig
    mx = jnp.maximum(log_lb, t2)
    g = mx + jnp.log(1.0 + jnp.exp(jnp.minimum(log_lb, t2) - mx))
    kin = (1.0 - lb) * sig_neg

    g2 = g * LOG2E
    cums = _dot01_l(l_cum, jnp.concatenate([rows(g2, c) for c in range(nchunk)], axis=1))
    bw = jnp.concatenate([cums[0:lc, c * dk:(c + 1) * dk] for c in range(nchunk)], axis=0)
    beta = jnp.concatenate([cums[lc:2 * lc, c * dk:(c + 1) * dk] for c in range(nchunk)], axis=0)
    bfull = bw + beta
    qd = qf * jnp.exp2(bw)
    qd_b = qd.astype(BF16)
    qs_b = (qd * jnp.exp2(beta)).astype(BF16)
    blast = [bfull[(c + 1) * lc - 1:(c + 1) * lc, :] for c in range(nchunk)]
    blast_rows = jnp.concatenate([jnp.broadcast_to(b, (lc, dk)) for b in blast], axis=0)
    kd_b = (kin * jnp.exp2(blast_rows - bfull)).astype(BF16)

    upd = [_dot_tn(rows(v_b, c), rows(kd_b, c)) for c in range(nchunk)]
    kcats = []
    vcats = []
    for c in range(nchunk):
        kin_c, beta_c, bf_c, vb_c = rows(kin, c), rows(beta, c), rows(bfull, c), rows(v_b, c)
        kcat = []
        vcat = []
        for i in range(1, nsub):
            beta_i = beta_c[i * sub:i * sub + 1, :]
            kcat.append(kin_c[0:i * sub, :] * jnp.exp2(beta_i - bf_c[0:i * sub, :]))
            vcat.append(vb_c[0:i * sub, :])
        kcats.append(jnp.concatenate(kcat, axis=0).astype(BF16))
        vcats.append(jnp.concatenate(vcat, axis=0))
    a_raw = [_dot_nt(rows(qd_b, c), kcats[c]) for c in range(nchunk)]
    o_cat = [_dot(jnp.where(cat_mask, a_raw[c], 0.0).astype(BF16), vcats[c]) for c in range(nchunk)]

    nblk = tb // sub
    row_ref[0] = bw
    row_ref[1] = kin
    row_ref[2] = v
    prods = []
    for blk in range(nblk):
        sl = slice(blk * sub, (blk + 1) * sub)
        q_s, b_s = qf[sl], bw[sl]
        for s in range(sub):
            r = blk * sub + s
            diff = b_s - row_ref[0, r:r + 1, :]
            if s > 0:
                diff = jnp.where(tsub >= s, diff, -jnp.inf)
            prods.append(q_s * row_ref[1, r:r + 1, :] * jnp.exp2(diff))
    rs = _dot(jnp.concatenate(prods, axis=0).astype(BF16), ones_b)
    o_diag = []
    for blk in range(nblk):
        base = blk * sub * sub
        r = blk * sub
        od = rs[base:base + sub, :] * row_ref[2, r:r + 1, :]
        for s in range(1, sub):
            od = od + rs[base + s * sub:base + (s + 1) * sub, :] * row_ref[2, r + s:r + s + 1, :]
        o_diag.append(od)

    st = st_ref[...]
    o_state = []
    for c in range(nchunk):
        o_state.append(_dot_nt(rows(qs_b, c), st.astype(BF16)))
        st = st * jnp.exp2(blast[c]) + upd[c]
    st_ref[...] = st

    o = jnp.concatenate(o_state, axis=0) + jnp.concatenate(o_cat, axis=0) + jnp.concatenate(o_diag, axis=0)
    ms = jnp.mean(o * o, axis=-1, keepdims=True)
    o = o * lax.rsqrt(ms + NORM_EPS) * nw_ref[...]
    o = o * _silu(gt_ref[...].astype(F32))
    o_ref[...] = o.astype(o_ref.dtype)


def _hgrn(big, fgate, lb_logits, norm_w, layer, bsz, seq, tb=1024):
    n = big.shape[0]
    nt = seq // tb
    nl = lb_logits.shape[0]
    qb, ib, gb = OD_Q // LANE, OD_I // LANE, OD_G // LANE
    row = lambda b, h, t: b * nt + t
    return pl.pallas_call(
        functools.partial(_hgrn_kernel, tb=tb, layer=layer),
        out_shape=jax.ShapeDtypeStruct((n, HGRN_WIDTH), BF16),
        grid=(bsz, HGRN_HEADS, nt),
        in_specs=[pl.BlockSpec((tb, LANE), lambda b, h, t: (row(b, h, t), qb + h)),
                  pl.BlockSpec((tb, LANE), lambda b, h, t: (row(b, h, t), ib + h)),
                  pl.BlockSpec((tb, LANE), lambda b, h, t: (row(b, h, t), gb + h)),
                  pl.BlockSpec((tb, LANE), lambda b, h, t: (row(b, h, t), h)),
                  pl.BlockSpec((nl, LANE), lambda b, h, t: (0, h)),
                  pl.BlockSpec((1, LANE), lambda b, h, t: (0, 0))],
        out_specs=pl.BlockSpec((tb, LANE), lambda b, h, t: (row(b, h, t), h)),
        scratch_shapes=[pltpu.VMEM((HGRN_DK, HGRN_DK), F32), pltpu.VMEM((3, tb, LANE), F32)],
        compiler_params=_cparams("parallel", "parallel", "arbitrary"),
        name="hgrn2_scan",
    )(big, big, big, fgate, lb_logits.astype(F32), norm_w.astype(F32).reshape(1, LANE))


def _even_weights(w_in):
    w = w_in.astype(F32)
    q, k, v, ga = (w[:, i * 1024:(i + 1) * 1024] for i in range(4))
    q = q * (FOX_HEAD_DIM ** -0.5 * LOG2E)
    fl = w[:, 4096:4104]
    z = w[:, 4104:5128]
    xbc = w[:, 5128:6664]
    dt = w[:, 6664:6680]
    main = jnp.concatenate([q, k, v, ga, z, xbc], axis=1).astype(BF16)
    small = jnp.zeros((D_MODEL, 3 * LANE), F32).at[:, 0:8].set(fl)
    small = small.at[:, LANE:LANE + 8].set(dt[:, 0:8]).at[:, 2 * LANE:2 * LANE + 8].set(dt[:, 8:16])
    small_t = jnp.zeros((32, D_MODEL), F32).at[0:8].set(fl.T).at[8:24].set(dt.T)
    return main, small.astype(BF16), small_t.astype(BF16)


def _even_layer(h, u, w_in, w_out, f_bias, conv_w, conv_b, dt_bias, a_log, d_skip, ssd_nw,
                next_norm_w, final, bsz, seq):
    w_main, w_small, w_small_t = _even_weights(w_in)
    big = _matmul(u, w_main, BF16, tm=1024, tn=512, name="even_in_proj")
    small, small_t = _small_proj(u, w_small, w_small_t)
    k_aug, q_aug = _fox_gate_cumsum(small, f_bias, bsz, seq)
    o_a = _fox_attention(big, k_aug, q_aug, bsz, seq)
    y = _ssd(big, small, small_t, conv_w, conv_b, dt_bias, a_log, d_skip, ssd_nw, bsz, seq)
    return _out_proj([o_a, y], w_out.astype(BF16), h, next_norm_w, final)


def _odd_layer(h, u, w_in, w_out, lb_logits, hgrn_nw, layer, next_norm_w, final, bsz, seq):
    w = w_in.astype(F32)
    w_main = jnp.concatenate([w[:, 0:2048], w[:, 4096:6144], w[:, 6144:8192]], axis=1).astype(BF16)
    w_f = w[:, 2048:4096].astype(BF16)
    big = _matmul(u, w_main, BF16, tm=1024, tn=512, name="odd_in_proj")
    fgate = _matmul(u, w_f, F32, tm=1024, tn=512, name="odd_gate_proj")
    o = _hgrn(big, fgate, lb_logits, hgrn_nw, layer, bsz, seq)
    return _out_proj([o], w_out.astype(BF16), h, next_norm_w, final)


def kernel(x, norm_w, final_norm_w, even_w_in, even_w_out, fox_f_bias, ssd_conv_w, ssd_conv_b, ssd_dt_bias,
           ssd_A_log, ssd_D, ssd_norm_w, odd_w_in, odd_w_out, hgrn_lb_logits, hgrn_norm_w):
    bsz, seq, d = x.shape
    depth = norm_w.shape[0]
    assert d == D_MODEL and seq % 1024 == 0, (x.shape,)
    h = x.reshape(bsz * seq, d).astype(F32)
    u = _rmsnorm(h, norm_w[0].astype(F32))
    out = None
    for layer in range(depth):
        final = layer == depth - 1
        next_w = (final_norm_w if final else norm_w[layer + 1]).astype(F32)
        if layer % 2 == 0:
            e = layer // 2
            res = _even_layer(h, u, even_w_in[e], even_w_out[e], fox_f_bias[e], ssd_conv_w[e], ssd_conv_b[e],
                              ssd_dt_bias[e], ssd_A_log[e], ssd_D[e], ssd_norm_w[e], next_w, final, bsz, seq)
        else:
            o = layer // 2
            res = _odd_layer(h, u, odd_w_in[o], odd_w_out[o], hgrn_lb_logits, hgrn_norm_w[o], o,
                             next_w, final, bsz, seq)
        if final:
            out = res
        else:
            h, u = res
    return out.reshape(bsz, seq, d).astype(x.dtype)
```
